```python
import jax, jax.numpy as jnp
from jax import lax
import numpy as np

D_MODEL = 1024
BATCH = 2
SEQ = 8192
DEPTH = 4
DEC_BATCH = 128
DEC_SEQ = 1
PAST_LEN = 2048
PAGE_SIZE = 128

N_A = DEPTH // 2
N_B = DEPTH - N_A
MIX_HALF = D_MODEL // 2
HG_HEADS = 4
HG_DK = MIX_HALF // HG_HEADS
HG_DV = MIX_HALF // HG_HEADS
HG_CHUNK = 64
SB_HEADS = 4
SB_DH = MIX_HALF // SB_HEADS
SB_BLOCK = 128
SB_BIAS_INIT = -8.0
MEM_HEADS = 4
MEM_DH = MIX_HALF // MEM_HEADS
N_MEM = 256
D_FF = ((8 * D_MODEL + 3 * 256 - 1) // (3 * 256)) * 256
EPS = 1e-6

kernel_name = 'yoco_hgrn2_stickbreak_mem_decoder_step'


def rmsnorm(x, g):
    xf = x.astype(jnp.float32)
    y = xf * lax.rsqrt(jnp.mean(xf * xf, axis=-1, keepdims=True) + EPS)
    return (y * g.astype(jnp.float32)).astype(x.dtype)


def swiglu(h, w_gu, w_down):
    gate, up = jnp.split(h @ w_gu, 2, axis=-1)
    return (jax.nn.silu(gate) * up) @ w_down


def hgrn_lower_bounds(lb_param):
    c = jnp.cumsum(jax.nn.softmax(lb_param.astype(jnp.float32), axis=0), axis=0)
    return c - c[0:1]


def hgrn2_scan(q, k, v, logf, s0):
    B, L, H, DK = q.shape
    DV = v.shape[-1]
    C = min(HG_CHUNK, L)
    pad = (-L) % C
    q, k, v, logf = (a.astype(jnp.float32) for a in (q, k, v, logf))
    if pad:
        pw = ((0, 0), (0, pad), (0, 0), (0, 0))
        q, k, v, logf = (jnp.pad(a, pw) for a in (q, k, v, logf))
    n = (L + pad) // C

    def to_chunks(a):
        return a.reshape(B, n, C, H, a.shape[-1]).transpose(1, 0, 3, 2, 4)

    causal = jnp.tril(jnp.ones((C, C), bool))[None, None, :, :, None]

    def step(S, inp):
        qi, ki, vi, fi = inp
        b = jnp.cumsum(fi, axis=2)
        o_inter = jnp.einsum('bhtk,bhkv->bhtv', qi * jnp.exp(b), S)
        rel = b[:, :, :, None, :] - b[:, :, None, :, :]
        dec = jnp.exp(jnp.where(causal, rel, -jnp.inf))
        att = jnp.einsum('bhtk,bhsk,bhtsk->bhts', qi, ki, dec)
        o = o_inter + jnp.einsum('bhts,bhsv->bhtv', att, vi)
        b_last = b[:, :, -1:, :]
        S_new = jnp.exp(b_last[:, :, 0, :, None]) * S + jnp.einsum('bhsk,bhsv->bhkv', ki * jnp.exp(b_last - b), vi)
        return S_new, o

    S, o = lax.scan(step, s0.astype(jnp.float32), tuple(to_chunks(a) for a in (q, k, v, logf)))
    o = o.transpose(1, 0, 3, 2, 4).reshape(B, n * C, H, DV)[:, :L]
    return o, S


def stick_breaking(q, k, v, bias, q_pos, k_pos):
    B, Lq, H, dh = q.shape
    blk = min(SB_BLOCK, Lq)
    pad = (-Lq) % blk
    qq = q
    if pad:
        qq = jnp.pad(qq, ((0, 0), (0, pad), (0, 0), (0, 0)))
        q_pos = jnp.pad(q_pos, (0, pad))
    nb = (Lq + pad) // blk
    qb = qq.reshape(B, nb, blk, H, dh).transpose(1, 0, 2, 3, 4)
    pb = q_pos.reshape(nb, blk)
    kf = k.astype(jnp.float32)
    vf = v.astype(jnp.float32)
    bf = bias.astype(jnp.float32)[None, :, None, None]
    scale = dh ** -0.5

    def one_block(args):
        qi, pi = args
        z = jnp.einsum('bqhd,bkhd->bhqk', qi.astype(jnp.float32), kf) * scale + bf
        mask = (k_pos[None, :] < pi[:, None])[None, None]
        log_keep = jnp.where(mask, jax.nn.log_sigmoid(-z), 0.0)
        tail = lax.cumsum(log_keep, axis=3, reverse=True) - log_keep
        w = jnp.where(mask, jnp.exp(jax.nn.log_sigmoid(z) + tail), 0.0)
        return jnp.einsum('bhqk,bkhd->bqhd', w, vf)

    o = lax.map(one_block, (qb, pb))
    o = o.transpose(1, 0, 2, 3, 4).reshape(B, nb * blk, H, dh)[:, :Lq]
    return o.astype(q.dtype)


def mem_attend(q, mk, mv):
    s = jnp.einsum('blhd,bmhd->bhlm', q.astype(jnp.float32), mk.astype(jnp.float32)) * (q.shape[-1] ** -0.5)
    p = jax.nn.softmax(s, axis=-1)
    return jnp.einsum('bhlm,bmhd->blhd', p, mv.astype(jnp.float32)).astype(q.dtype)


def a_mixer(hn, w_in, lb, hg_gain, mem_k, mem_v, w_o, s0):
    B, L, _ = hn.shape
    q_h, f_h, i_h, g_h, q_m = jnp.split(hn @ w_in, 5, axis=-1)
    q = jax.nn.silu(q_h).reshape(B, L, HG_HEADS, HG_DK)
    logf = jnp.logaddexp(jnp.log(lb), jnp.log1p(-lb) + jax.nn.log_sigmoid(f_h.astype(jnp.float32)))
    logf = logf.reshape(B, L, HG_HEADS, HG_DK)
    k = -jnp.expm1(logf)
    v = i_h.reshape(B, L, HG_HEADS, HG_DV)
    o, S = hgrn2_scan(q, k, v, logf, s0)
    o = rmsnorm(o.astype(hn.dtype), hg_gain) * jax.nn.silu(g_h).reshape(B, L, HG_HEADS, HG_DV)
    om = mem_attend(q_m.reshape(B, L, MEM_HEADS, MEM_DH), mem_k, mem_v)
    out = jnp.concatenate([o.reshape(B, L, MIX_HALF), om.reshape(B, L, MIX_HALF)], axis=-1) @ w_o
    return out, S


def b_mixer(hn, w_in, bias, k_all, v_all, q_pos, k_pos, mem_k, mem_v, w_o):
    B, L, _ = hn.shape
    q_s, q_m = jnp.split(hn @ w_in, 2, axis=-1)
    os_ = stick_breaking(q_s.reshape(B, L, SB_HEADS, SB_DH), k_all, v_all, bias, q_pos, k_pos)
    om = mem_attend(q_m.reshape(B, L, MEM_HEADS, MEM_DH), mem_k, mem_v)
    return jnp.concatenate([os_.reshape(B, L, MIX_HALF), om.reshape(B, L, MIX_HALF)], axis=-1) @ w_o


def run_trunk(x, mem_k, mem_v, hg_s0, past_k, past_v,
              g_norm, w_in_a, hg_lb, hg_norm, w_in_b, sb_bias, g_kv, w_kv, w_o, w_gu, w_down):
    B, L, _ = x.shape
    past = 0 if past_k is None else past_k.shape[1]
    lbs = hgrn_lower_bounds(hg_lb)
    q_pos = past + jnp.arange(L, dtype=jnp.int32)
    k_pos = jnp.arange(past + L, dtype=jnp.int32)
    h = x
    hg_states = []
    sb_k = sb_v = k_all = v_all = None
    for l in range(DEPTH):
        g = g_norm[l]
        hn = rmsnorm(h, g[0])
        if l < N_A:
            mix, S = a_mixer(hn, w_in_a[l], lbs[l], hg_norm[l], mem_k[l], mem_v[l], w_o[l], hg_s0[l])
            hg_states.append(S.astype(x.dtype))
        else:
            mix = b_mixer(hn, w_in_b[l - N_A], sb_bias[l - N_A], k_all, v_all, q_pos, k_pos,
                          mem_k[l], mem_v[l], w_o[l])
        h = h + rmsnorm(mix, g[1])
        h = h + rmsnorm(swiglu(rmsnorm(h, g[2]), w_gu[l], w_down[l]), g[3])
        if l == N_A - 1:
            sb_k, sb_v = jnp.split(rmsnorm(h, g_kv) @ w_kv, 2, axis=-1)
            sb_k = sb_k.reshape(B, L, SB_HEADS, SB_DH)
            sb_v = sb_v.reshape(B, L, SB_HEADS, SB_DH)
            if past_k is None:
                k_all, v_all = sb_k, sb_v
            else:
                k_all = jnp.concatenate([past_k.astype(sb_k.dtype), sb_k], axis=1)
                v_all = jnp.concatenate([past_v.astype(sb_v.dtype), sb_v], axis=1)
    return h, jnp.stack(hg_states), sb_k, sb_v


def setup_inputs(seed: int = 0) -> dict:
    key = jax.random.key(seed)
    ks = jax.random.split(key, 24)
    f32 = jnp.float32
    n_pages = PAST_LEN // PAGE_SIZE
    n_used = DEC_BATCH * n_pages
    n_pool = n_used + max(1, n_used // 4)

    def nrm(k, shape, s=1.0):
        return jax.random.normal(k, shape, f32) * s

    page_table = jax.random.permutation(ks[8], n_pool)[:n_used].reshape(DEC_BATCH, n_pages).astype(jnp.int32)
    return {
        'x_prompt': nrm(ks[0], (BATCH, SEQ, D_MODEL)),
        'x_sample': nrm(ks[1], (DEC_BATCH, DEC_SEQ, D_MODEL)),
        'mem_prompt': nrm(ks[2], (BATCH, N_MEM, D_MODEL)),
        'cache_sb_k': nrm(ks[3], (n_pool, PAGE_SIZE, SB_HEADS, SB_DH)),
        'cache_sb_v': nrm(ks[4], (n_pool, PAGE_SIZE, SB_HEADS, SB_DH)),
        'cache_mem_k': nrm(ks[5], (DEPTH, DEC_BATCH, N_MEM, MEM_HEADS, MEM_DH)),
        'cache_mem_v': nrm(ks[6], (DEPTH, DEC_BATCH, N_MEM, MEM_HEADS, MEM_DH)),
        'state_hgrn': nrm(ks[7], (N_A, DEC_BATCH, HG_HEADS, HG_DK, HG_DV), 0.5),
        'page_table': page_table,
        'g_norm': 1.0 + nrm(ks[9], (DEPTH, 4, D_MODEL), 0.05),
        'w_in_a': nrm(ks[10], (N_A, D_MODEL, 5 * MIX_HALF), D_MODEL ** -0.5),
        'hg_lb': nrm(ks[11], (N_A, HG_HEADS * HG_DK), 0.5),
        'hg_norm': 1.0 + nrm(ks[12], (N_A, HG_DV), 0.05),
        'w_in_b': nrm(ks[13], (N_B, D_MODEL, 2 * MIX_HALF), D_MODEL ** -0.5),
        'sb_bias': SB_BIAS_INIT + nrm(ks[20], (N_B, SB_HEADS), 0.3),
        'g_kv': 1.0 + nrm(ks[14], (D_MODEL,), 0.05),
        'w_kv': nrm(ks[15], (D_MODEL, 2 * SB_HEADS * SB_DH), D_MODEL ** -0.5),
        'w_mem_kv': nrm(ks[16], (DEPTH, D_MODEL, 2 * MEM_HEADS * MEM_DH), D_MODEL ** -0.5),
        'w_o': nrm(ks[17], (DEPTH, 2 * MIX_HALF, D_MODEL), (2 * MIX_HALF) ** -0.5),
        'w_gu': nrm(ks[18], (DEPTH, D_MODEL, 2 * D_FF), D_MODEL ** -0.5),
        'w_down': nrm(ks[19], (DEPTH, D_FF, D_MODEL), D_FF ** -0.5),
    }


def reference(x_prompt, x_sample, mem_prompt, cache_sb_k, cache_sb_v, cache_mem_k, cache_mem_v, state_hgrn,
              page_table, g_norm, w_in_a, hg_lb, hg_norm, w_in_b, sb_bias, g_kv, w_kv, w_mem_kv, w_o, w_gu, w_down):
    weights = (g_norm, w_in_a, hg_lb, hg_norm, w_in_b, sb_bias, g_kv, w_kv, w_o, w_gu, w_down)
    bp = mem_prompt.shape[0]
    mk_p, mv_p = jnp.split(jnp.einsum('bmd,lde->lbme', mem_prompt, w_mem_kv), 2, axis=-1)
    mem_k_prompt = mk_p.reshape(DEPTH, bp, N_MEM, MEM_HEADS, MEM_DH)
    mem_v_prompt = mv_p.reshape(DEPTH, bp, N_MEM, MEM_HEADS, MEM_DH)
    s0_p = jnp.zeros((N_A, bp, HG_HEADS, HG_DK, HG_DV), jnp.float32)
    y_prompt, state_hgrn_prompt, sb_k_prompt, sb_v_prompt = run_trunk(
        x_prompt, mem_k_prompt, mem_v_prompt, s0_p, None, None, *weights)
    ds, n_pages = page_table.shape
    past_k = cache_sb_k[page_table].reshape(ds, n_pages * PAGE_SIZE, SB_HEADS, SB_DH)
    past_v = cache_sb_v[page_table].reshape(ds, n_pages * PAGE_SIZE, SB_HEADS, SB_DH)
    y_sample, state_hgrn_sample, sb_k_sample, sb_v_sample = run_trunk(
        x_sample, cache_mem_k, cache_mem_v, state_hgrn, past_k, past_v, *weights)
    return (y_prompt, y_sample, state_hgrn_prompt, state_hgrn_sample.astype(state_hgrn.dtype),
            sb_k_prompt, sb_v_prompt, sb_k_sample, sb_v_sample, mem_k_prompt, mem_v_prompt)
```

```python
import functools

import numpy as np
import jax
import jax.numpy as jnp
from jax import lax
from jax.experimental import pallas as pl
from jax.experimental.pallas import tpu as pltpu

F32 = jnp.float32
BF16 = jnp.bfloat16
EPS = 1e-6
HEADS = 4
DH = 128
MIX = HEADS * DH
CHUNK = 64
SUB = 8
SB_BLK = 128
V7X_VMEM_BYTES = 64 * 1024 * 1024
VMEM_LIMIT = V7X_VMEM_BYTES - 8 * 1024 * 1024


def _cp(*sem):
    return pltpu.CompilerParams(dimension_semantics=sem, vmem_limit_bytes=VMEM_LIMIT)


def _dot(a, b):
    return jnp.dot(a, b, preferred_element_type=F32)


def _dot_nt(a, b):
    return lax.dot_general(a, b, (((1,), (1,)), ((), ())), preferred_element_type=F32)


def _rms(x, g):
    return x * lax.rsqrt(jnp.mean(x * x, axis=-1, keepdims=True) + EPS) * g


def _sigmoid(x):
    return 1.0 / (1.0 + jnp.exp(-x))


def _log_sigmoid(x):
    return jnp.minimum(x, 0.0) - jnp.log1p(jnp.exp(-jnp.abs(x)))


def _resident(shape):
    return pl.BlockSpec(shape, lambda *_: (0,) * len(shape), pipeline_mode=pl.Buffered(1))


def _memkv_kernel(x_ref, w_ref, k_ref, v_ref):
    y = _dot(x_ref[...].astype(BF16), w_ref[...].astype(BF16))
    k_ref[...] = y[:, :MIX]
    v_ref[...] = y[:, MIX:]


def _memkv_proj(mem, w_mem_kv):
    depth = w_mem_kv.shape[0]
    b, n_mem, d = mem.shape
    rows = b * n_mem
    out = jax.ShapeDtypeStruct((depth, rows, MIX), F32)
    k, v = pl.pallas_call(
        _memkv_kernel,
        grid=(depth,),
        in_specs=[pl.BlockSpec((rows, d), lambda l: (0, 0)),
                  pl.BlockSpec((None, d, 2 * MIX), lambda l: (l, 0, 0))],
        out_specs=[pl.BlockSpec((None, rows, MIX), lambda l: (l, 0, 0))] * 2,
        out_shape=[out, out],
        compiler_params=_cp("arbitrary"),
        name="memkv_proj",
    )(mem.reshape(rows, d), w_mem_kv)
    return k.reshape(depth, b, n_mem, MIX), v.reshape(depth, b, n_mem, MIX)


def _norm_matmul_kernel(x_ref, g_ref, w_ref, *out_refs):
    xn = _rms(x_ref[...], g_ref[...]).astype(BF16)
    off = 0
    for o_ref in out_refs:
        n = o_ref.shape[-1]
        o_ref[...] = _dot(xn, w_ref[:, off:off + n])
        off += n


def _row_tile(m):
    return min(m, 512)


def _norm_matmul(x, g, w, widths, name):
    m, d = x.shape
    tm = _row_tile(m)
    return pl.pallas_call(
        _norm_matmul_kernel,
        grid=(m // tm,),
        in_specs=[pl.BlockSpec((tm, d), lambda i: (i, 0)),
                  _resident((1, d)),
                  _resident(w.shape)],
        out_specs=[pl.BlockSpec((tm, n), lambda i: (i, 0)) for n in widths],
        out_shape=[jax.ShapeDtypeStruct((m, n), F32) for n in widths],
        compiler_params=_cp("arbitrary"),
        name=name,
    )(x, g.reshape(1, d), w)


def _inproj_a_kernel(x_ref, g_ref, lb_ref, w_ref, o_ref):
    xn = _rms(x_ref[...], g_ref[...]).astype(BF16)
    q = _dot(xn, w_ref[:, 0:MIX])
    o_ref[:, 0:MIX] = q * _sigmoid(q)
    f = _dot(xn, w_ref[:, MIX:2 * MIX])
    lb = lb_ref[...]
    a = jnp.log(lb)
    b = jnp.log1p(-lb) + _log_sigmoid(f)
    o_ref[:, MIX:2 * MIX] = jnp.maximum(a, b) + jnp.log1p(jnp.exp(-jnp.abs(a - b)))
    o_ref[:, 2 * MIX:3 * MIX] = _dot(xn, w_ref[:, 2 * MIX:3 * MIX])
    gt = _dot(xn, w_ref[:, 3 * MIX:4 * MIX])
    o_ref[:, 3 * MIX:4 * MIX] = gt * _sigmoid(gt)
    o_ref[:, 4 * MIX:5 * MIX] = _dot(xn, w_ref[:, 4 * MIX:5 * MIX])


def _inproj_a(x, g, lb, w):
    m, d = x.shape
    tm = _row_tile(m)
    n = w.shape[1]
    return pl.pallas_call(
        _inproj_a_kernel,
        grid=(m // tm,),
        in_specs=[pl.BlockSpec((tm, d), lambda i: (i, 0)),
                  _resident((1, d)),
                  _resident((1, MIX)),
                  _resident(w.shape)],
        out_specs=pl.BlockSpec((tm, n), lambda i: (i, 0)),
        out_shape=jax.ShapeDtypeStruct((m, n), F32),
        compiler_params=_cp("arbitrary"),
        name="inproj_a",
    )(x, g.reshape(1, d), lb.reshape(1, MIX), w)


def _hgrn_consts():
    c = CHUNK
    t = np.arange(c)[:, None]
    j = np.arange(c)[None, :]
    mats = [j <= t]
    levels = []
    for m in (SUB, 2 * SUB, 4 * SUB):
        base = (t // (2 * m)) * (2 * m)
        lower = (t - base) >= m
        mats.append((lower & (j >= base + m) & (j <= t)) | (~lower & (j >= t + 1) & (j <= base + m - 1)))
        levels.append(((t // (2 * m)) == (j // (2 * m))) & ((t % (2 * m)) >= m) & ((j % (2 * m)) < m))
    mats.append(j >= t + 1)
    diags = [(j == t)]
    for d in range(1, SUB):
        mats.append(((t % SUB) >= d) & (j >= t - d + 1) & (j <= t))
        diags.append((j == t - d) & ((t % SUB) >= d))
    cm = np.concatenate(mats, 0).astype(np.float32)
    masks = np.stack(levels + diags).astype(np.float32)
    return jnp.asarray(cm, BF16), jnp.asarray(masks, F32)


def _hgrn_chunk(q, lf, v, st, cm, mk_ref, ones_bf):
    c = CHUNK
    hi = lf.astype(BF16)
    r1 = lf - hi.astype(F32)
    mid = r1.astype(BF16)
    lo = (r1 - mid.astype(F32)).astype(BF16)
    x = jnp.exp(_dot(cm, hi) + _dot(cm, mid) + _dot(cm, lo))
    k = 1.0 - jnp.exp(lf)
    eb = x[0:c]
    o = _dot_nt((q * eb).astype(BF16), st.astype(BF16))
    att = mk_ref[3] * _dot((q * k).astype(BF16), ones_bf)
    for li in range(3):
        a = x[c * (1 + li):c * (2 + li)]
        att = att + mk_ref[li] * _dot_nt((q * a).astype(BF16), (k * a).astype(BF16))
    for d in range(1, SUB):
        gd = x[c * (4 + d):c * (5 + d)]
        p = q * pltpu.roll(k, d, 0) * gd
        att = att + mk_ref[3 + d] * _dot(p.astype(BF16), ones_bf)
    o = o + _dot(att.astype(BF16), v.astype(BF16))
    kh = (k * x[4 * c:5 * c]).astype(BF16)
    st_new = st * eb[c - 1:c, :] + _dot(v.T.astype(BF16), kh)
    return o, st_new


def _hgrn_scan_kernel(q_ref, lf_ref, v_ref, g_ref, gain_ref, cm_ref, mk_ref, on_ref, s_ref, st_ref, *, n_chunks):
    l = pl.program_id(2)

    @pl.when(l == 0)
    def _():
        st_ref[...] = jnp.zeros_like(st_ref)

    st = st_ref[...]
    cm = cm_ref[...]
    ones_bf = jnp.ones((DH, CHUNK), BF16)
    gain = gain_ref[...]
    for ci in range(n_chunks):
        sl = slice(ci * CHUNK, (ci + 1) * CHUNK)
        o, st = _hgrn_chunk(q_ref[sl, :], lf_ref[sl, :], v_ref[sl, :], st, cm, mk_ref, ones_bf)
        on_ref[sl, :] = _rms(o, gain) * g_ref[sl, :]
    st_ref[...] = st

    @pl.when(l == pl.num_programs(2) - 1)
    def _():
        s_ref[...] = st.T


def _hgrn_scan(y, gain, tokens_per_step=256):
    b, l, _ = y.shape
    t = min(tokens_per_step, l)
    assert l % t == 0 and t % CHUNK == 0
    cm, masks = _hgrn_consts()

    def col(group):
        return pl.BlockSpec((None, t, DH), lambda bi, hi, li: (bi, li, group * HEADS + hi))

    return pl.pallas_call(
        functools.partial(_hgrn_scan_kernel, n_chunks=t // CHUNK),
        grid=(b, HEADS, l // t),
        in_specs=[col(0), col(1), col(2), col(3),
                  _resident((1, DH)), _resident(cm.shape), _resident(masks.shape)],
        out_specs=[pl.BlockSpec((None, t, DH), lambda bi, hi, li: (bi, li, hi)),
                   pl.BlockSpec((None, None, DH, DH), lambda bi, hi, li: (bi, hi, 0, 0))],
        out_shape=[jax.ShapeDtypeStruct((b, l, MIX), F32),
                   jax.ShapeDtypeStruct((b, HEADS, DH, DH), F32)],
        scratch_shapes=[pltpu.VMEM((DH, DH), F32)],
        compiler_params=_cp("arbitrary", "arbitrary", "arbitrary"),
        name="hgrn_scan",
    )(y, y, y, y, gain.reshape(1, DH), cm, masks)


def _hgrn_step_kernel(q_ref, lf_ref, v_ref, g_ref, gain_ref, s_ref, on_ref, so_ref, o_scr, *, n_samples):
    lf = lf_ref[...]
    f = jnp.exp(lf)
    ft = f.T
    kt = (1.0 - f).T
    qt = q_ref[...].T
    v = v_ref[...]
    for i in range(n_samples):
        sn = ft[:, i:i + 1] * s_ref[i] + kt[:, i:i + 1] * v[i:i + 1, :]
        so_ref[i] = sn
        o_scr[i:i + 1, :] = jnp.sum(qt[:, i:i + 1] * sn, axis=0, keepdims=True)
    on_ref[...] = _rms(o_scr[...], gain_ref[...]) * g_ref[...]


def _hgrn_step(y, gain, state, layer):
    n = y.shape[0]

    def col(group):
        return pl.BlockSpec((n, DH), lambda hi: (0, group * HEADS + hi))

    return pl.pallas_call(
        functools.partial(_hgrn_step_kernel, n_samples=n),
        grid=(HEADS,),
        in_specs=[col(0), col(1), col(2), col(3), _resident((1, DH)),
                  pl.BlockSpec((None, n, None, DH, DH), lambda hi: (layer, 0, hi, 0, 0))],
        out_specs=[pl.BlockSpec((n, DH), lambda hi: (0, hi)),
                   pl.BlockSpec((n, None, DH, DH), lambda hi: (0, hi, 0, 0))],
        out_shape=[jax.ShapeDtypeStruct((n, MIX), F32),
                   jax.ShapeDtypeStruct(state.shape[1:], F32)],
        scratch_shapes=[pltpu.VMEM((n, DH), F32)],
        compiler_params=_cp("arbitrary"),
        name="hgrn_step",
    )(y, y, y, y, gain.reshape(1, DH), state)


def _mix_out_prompt_kernel(h_ref, a_ref, qm_ref, mk_ref, mv_ref, wo_ref, g_ref, o_ref, *, scale):
    mix = _dot(a_ref[...].astype(BF16), wo_ref[0:MIX, :])
    for hd in range(HEADS):
        sl = slice(hd * DH, (hd + 1) * DH)
        s = _dot_nt((qm_ref[:, sl] * scale).astype(BF16), mk_ref[:, sl].astype(BF16))
        p = jnp.exp(s - jnp.max(s, axis=-1, keepdims=True))
        om = _dot(p.astype(BF16), mv_ref[:, sl].astype(BF16)) / jnp.sum(p, axis=-1, keepdims=True)
        mix = mix + _dot(om.astype(BF16), wo_ref[MIX + hd * DH:MIX + (hd + 1) * DH, :])
    o_ref[...] = h_ref[...] + _rms(mix, g_ref[...])


def _mix_out_prompt(h, a, y, qm_col, mem_k, mem_v, w_o, g):
    b, l, d = h.shape
    tm = _row_tile(l)
    n_mem = mem_k.shape[1]
    tok = lambda w, c=0: pl.BlockSpec((None, tm, w), lambda bi, li: (bi, li, c))
    mem = pl.BlockSpec((None, n_mem, MIX), lambda bi, li: (bi, 0, 0))
    return pl.pallas_call(
        functools.partial(_mix_out_prompt_kernel, scale=DH ** -0.5),
        grid=(b, l // tm),
        in_specs=[tok(d), tok(MIX), tok(MIX, qm_col), mem, mem, _resident(w_o.shape), _resident((1, d))],
        out_specs=tok(d),
        out_shape=jax.ShapeDtypeStruct(h.shape, F32),
        compiler_params=_cp("arbitrary", "arbitrary"),
        name="mix_out_prompt",
    )(h, a, y, mem_k, mem_v, w_o, g.reshape(1, d))


def _head_rows(x_row, rows):
    r = lax.broadcasted_iota(jnp.int32, (rows, MIX), 0)
    hd = lax.shift_right_logical(lax.broadcasted_iota(jnp.int32, (rows, MIX), 1), DH.bit_length() - 1)
    sel = r == hd
    return jnp.where(sel, x_row, 0.0), sel


def _mix_out_sample_kernel(h_ref, a_ref, qm_ref, mk_ref, mv_ref, wo_ref, g_ref, o_ref, om_scr, *, scale, tb):
    for i in range(tb):
        qb, sel = _head_rows(qm_ref[i:i + 1, :] * scale, 8)
        s = _dot_nt(qb.astype(BF16), mk_ref[i].astype(BF16))
        p = jnp.exp(s - jnp.max(s, axis=-1, keepdims=True))
        full = _dot(p.astype(BF16), mv_ref[i].astype(BF16)) / jnp.sum(p, axis=-1, keepdims=True)
        om_scr[i:i + 1, :] = jnp.sum(jnp.where(sel, full, 0.0), axis=0, keepdims=True)
    mix = _dot(a_ref[...].astype(BF16), wo_ref[0:MIX, :]) + _dot(om_scr[...].astype(BF16), wo_ref[MIX:, :])
    o_ref[...] = h_ref[...] + _rms(mix, g_ref[...])


def _mix_out_sample(h, a, y, qm_col, mem_k, mem_v, layer, w_o, g):
    n, d = h.shape
    tb = min(n, 8)
    n_mem = mem_k.shape[2]
    tok = lambda w, c=0: pl.BlockSpec((tb, w), lambda i: (i, c))
    mem = pl.BlockSpec((None, tb, n_mem, MIX), lambda i: (layer, i, 0, 0))
    return pl.pallas_call(
        functools.partial(_mix_out_sample_kernel, scale=DH ** -0.5, tb=tb),
        grid=(n // tb,),
        in_specs=[tok(d), tok(MIX), tok(MIX, qm_col), mem, mem, _resident(w_o.shape), _resident((1, d))],
        out_specs=tok(d),
        out_shape=jax.ShapeDtypeStruct(h.shape, F32),
        scratch_shapes=[pltpu.VMEM((tb, MIX), F32)],
        compiler_params=_cp("arbitrary"),
        name="mix_out_sample",
    )(h, a, y, mem_k, mem_v, w_o, g.reshape(1, d))


def _ffn_kernel(h_ref, g_in_ref, wgu_ref, wd_ref, g_out_ref, o_ref, act_ref, *, d_ff, tc):
    x = h_ref[...]
    xn = _rms(x, g_in_ref[...]).astype(BF16)
    for c0 in range(0, d_ff, tc):
        gate = _dot(xn, wgu_ref[:, c0:c0 + tc])
        up = _dot(xn, wgu_ref[:, d_ff + c0:d_ff + c0 + tc])
        act_ref[:, c0:c0 + tc] = (gate * _sigmoid(gate) * up).astype(BF16)
    y = _dot(act_ref[...], wd_ref[...])
    o_ref[...] = x + _rms(y, g_out_ref[...])


def _ffn(h, g_in, w_gu, w_down, g_out):
    m, d = h.shape
    d_ff = w_down.shape[0]
    tm = _row_tile(m)
    tc = 256
    assert d_ff % tc == 0
    return pl.pallas_call(
        functools.partial(_ffn_kernel, d_ff=d_ff, tc=tc),
        grid=(m // tm,),
        in_specs=[pl.BlockSpec((tm, d), lambda i: (i, 0)),
                  _resident((1, d)), _resident(w_gu.shape), _resident(w_down.shape), _resident((1, d))],
        out_specs=pl.BlockSpec((tm, d), lambda i: (i, 0)),
        out_shape=jax.ShapeDtypeStruct((m, d), F32),
        scratch_shapes=[pltpu.VMEM((tm, d_ff), BF16)],
        compiler_params=_cp("arbitrary"),
        name="ffn",
    )(h, g_in.reshape(1, d), w_gu, w_down, g_out.reshape(1, d))


def _sb_consts():
    s = np.arange(SB_BLK)
    u = (s[:, None] > s[None, :]).astype(np.float32)
    return jnp.asarray(np.concatenate([u, np.ones((SB_BLK, SB_BLK), np.float32)], 1), BF16)


def _sb_block(z, c, u, mask):
    ls = _log_sigmoid(z)
    lk = ls - z
    if mask is not None:
        lk = jnp.where(mask, lk, 0.0)
    hi = lk.astype(BF16)
    lo = (lk - hi.astype(F32)).astype(BF16)
    cs = _dot(hi, u) + _dot(lo, u)
    w = jnp.exp(ls + c + cs[:, :SB_BLK])
    if mask is not None:
        w = jnp.where(mask, w, 0.0)
    return w, c + cs[:, SB_BLK:]


def _sb_prompt_kernel(bias_ref, q_ref, k_ref, v_ref, u_ref, o_ref, *, scale):
    hd = pl.program_id(1)
    i = pl.program_id(2)
    bias = bias_ref[hd]
    q = (q_ref[...] * scale).astype(BF16)
    u = u_ref[...]
    row = lax.broadcasted_iota(jnp.int32, (SB_BLK, SB_BLK), 0)
    col = lax.broadcasted_iota(jnp.int32, (SB_BLK, SB_BLK), 1)

    def step(j, acc, c, mask):
        off = pl.multiple_of(j * SB_BLK, SB_BLK)
        kj = k_ref[pl.ds(off, SB_BLK), :].astype(BF16)
        vj = v_ref[pl.ds(off, SB_BLK), :].astype(BF16)
        w, c = _sb_block(_dot_nt(q, kj) + bias, c, u, mask)
        return acc + _dot(w.astype(BF16), vj), c

    zero = jnp.zeros((SB_BLK, SB_BLK), F32)
    acc, c = step(i, zero, zero, col < row)
    acc, c = lax.fori_loop(0, i, lambda it, carry: step(i - 1 - it, carry[0], carry[1], None), (acc, c))
    o_ref[...] = acc


def _sb_prompt(yb, k, v, bias):
    b, l, _ = yb.shape
    assert l % SB_BLK == 0
    u = _sb_consts()
    kv = pl.BlockSpec((None, l, DH), lambda bi, hi, qi: (bi, 0, hi))
    qo = pl.BlockSpec((None, SB_BLK, DH), lambda bi, hi, qi: (bi, qi, hi))
    return pl.pallas_call(
        functools.partial(_sb_prompt_kernel, scale=DH ** -0.5),
        grid=(b, HEADS, l // SB_BLK),
        in_specs=[pl.BlockSpec(memory_space=pltpu.SMEM), qo, kv, kv, _resident(u.shape)],
        out_specs=qo,
        out_shape=jax.ShapeDtypeStruct((b, l, MIX), F32),
        compiler_params=_cp("arbitrary", "arbitrary", "arbitrary"),
        name="sb_prompt",
    )(bias, yb, k, v, u)


def _sb_sample_kernel(pt_ref, bias_ref, q_ref, *refs, n_pages, scale):
    k_refs = refs[:n_pages]
    v_refs = refs[n_pages:2 * n_pages]
    u_ref, o_ref = refs[2 * n_pages:]
    qb, sel = _head_rows(q_ref[...] * scale, 8)
    qb = qb.astype(BF16)
    u = u_ref[...]
    r = lax.broadcasted_iota(jnp.int32, (8, SB_BLK), 0)
    bias = jnp.zeros((8, SB_BLK), F32)
    for hd in range(HEADS):
        bias = jnp.where(r == hd, bias_ref[hd], bias)
    acc = jnp.zeros((8, MIX), F32)
    c = jnp.zeros((8, SB_BLK), F32)
    for j in reversed(range(n_pages)):
        z = _dot_nt(qb, k_refs[j][...].astype(BF16)) + bias
        w, c = _sb_block(z, c, u, None)
        acc = acc + _dot(w.astype(BF16), v_refs[j][...].astype(BF16))
    o_ref[...] = jnp.sum(jnp.where(sel, acc, 0.0), axis=0, keepdims=True)


def _sb_sample(yb, cache_k, cache_v, page_table, bias):
    n = yb.shape[0]
    n_pool, page = cache_k.shape[:2]
    assert page == SB_BLK
    n_pages = page_table.shape[1]
    u = _sb_consts()
    ck = cache_k.reshape(n_pool, page, MIX)
    cv = cache_v.reshape(n_pool, page, MIX)
    pages = [pl.BlockSpec((None, page, MIX), lambda i, pt, j=j: (pt[i, j], 0, 0)) for j in range(n_pages)]
    row = pl.BlockSpec((None, 1, MIX), lambda i, pt: (i, 0, 0))
    out = pl.pallas_call(
        functools.partial(_sb_sample_kernel, n_pages=n_pages, scale=DH ** -0.5),
        grid_spec=pltpu.PrefetchScalarGridSpec(
            num_scalar_prefetch=1,
            grid=(n,),
            in_specs=[pl.BlockSpec(memory_space=pltpu.SMEM), row] + pages + pages
                     + [pl.BlockSpec(u.shape, lambda i, pt: (0, 0))],
            out_specs=row),
        out_shape=jax.ShapeDtypeStruct((n, 1, MIX), F32),
        compiler_params=_cp("arbitrary"),
        name="sb_sample",
    )(page_table, bias, yb.reshape(n, 1, 2 * MIX), *([ck] * n_pages), *([cv] * n_pages), u)
    return out.reshape(n, MIX)


def _lower_bounds(lb_param):
    c = jnp.cumsum(jax.nn.softmax(lb_param.astype(F32), axis=0), axis=0)
    return c - c[0:1]


def _trunk(x, mem_k, mem_v, state, cache_k, cache_v, page_table, wts):
    g_norm, w_in_a, lbs, hg_norm, w_in_b, sb_bias, g_kv, w_kv, w_o, w_gu, w_down = wts
    b, l, d = x.shape
    sample = state is not None
    m = b * l
    depth = g_norm.shape[0]
    n_a = w_in_a.shape[0]
    h = x.reshape(m, d)
    states = []
    sb_k = sb_v = None
    for layer in range(depth):
        g = g_norm[layer]
        if layer < n_a:
            y = _inproj_a(h, g[0], lbs[layer], w_in_a[layer])
            if sample:
                a, s = _hgrn_step(y, hg_norm[layer], state, layer)
            else:
                a, s = _hgrn_scan(y.reshape(b, l, -1), hg_norm[layer])
            states.append(s)
            qm_col = 4
        else:
            (y,) = _norm_matmul(h, g[0], w_in_b[layer - n_a], (2 * MIX,), "inproj_b")
            if sample:
                a = _sb_sample(y, cache_k, cache_v, page_table, sb_bias[layer - n_a])
            else:
                a = _sb_prompt(y.reshape(b, l, -1), sb_k.reshape(b, l, MIX), sb_v.reshape(b, l, MIX),
                               sb_bias[layer - n_a])
            qm_col = 1
        if sample:
            h = _mix_out_sample(h, a.reshape(m, MIX), y, qm_col, mem_k, mem_v, layer, w_o[layer], g[1])
        else:
            h = _mix_out_prompt(h.reshape(b, l, d), a.reshape(b, l, MIX), y.reshape(b, l, -1), qm_col,
                                mem_k[layer], mem_v[layer], w_o[layer], g[1]).reshape(m, d)
        h = _ffn(h, g[2], w_gu[layer], w_down[layer], g[3])
        if layer == n_a - 1:
            sb_k, sb_v = _norm_matmul(h, g_kv, w_kv, (MIX, MIX), "kv_proj")
    return (h.reshape(b, l, d), jnp.stack(states),
            sb_k.reshape(b, l, HEADS, DH), sb_v.reshape(b, l, HEADS, DH))


def kernel(x_prompt, x_sample, mem_prompt, cache_sb_k, cache_sb_v, cache_mem_k, cache_mem_v, state_hgrn,
           page_table, g_norm, w_in_a, hg_lb, hg_norm, w_in_b, sb_bias, g_kv, w_kv, w_mem_kv, w_o, w_gu, w_down):
    depth = g_norm.shape[0]
    wts = (g_norm, w_in_a.astype(BF16), _lower_bounds(hg_lb), hg_norm, w_in_b.astype(BF16), sb_bias, g_kv,
           w_kv.astype(BF16), w_o.astype(BF16), w_gu.astype(BF16), w_down.astype(BF16))
    bp, n_mem = mem_prompt.shape[:2]
    mk_p, mv_p = _memkv_proj(mem_prompt, w_mem_kv)
    y_p, s_p, k_p, v_p = _trunk(x_prompt, mk_p, mv_p, None, None, None, None, wts)
    ds = x_sample.shape[0]
    cmk = cache_mem_k.reshape(depth, ds, n_mem, MIX)
    cmv = cache_mem_v.reshape(depth, ds, n_mem, MIX)
    y_s, s_s, k_s, v_s = _trunk(x_sample, cmk, cmv, state_hgrn, cache_sb_k, cache_sb_v, page_table, wts)
    return (y_p, y_s, s_p, s_s.astype(state_hgrn.dtype), k_p, v_p, k_s, v_s,
            mk_p.reshape(depth, bp, n_mem, HEADS, DH), mv_p.reshape(depth, bp, n_mem, HEADS, DH))
```

```python
import functools

import numpy as np
import jax
import jax.numpy as jnp
from jax import lax
from jax.experimental import pallas as pl
from jax.experimental.pallas import tpu as pltpu

F32 = jnp.float32
BF16 = jnp.bfloat16
EPS = 1e-6
LOG2E = 1.4426950408889634
HEADS = 4
DH = 128
MIX = HEADS * DH
CHUNK = 64
SUB = 8
SB_BLK = 128
SB_QTILE = 1024
SB_UNROLL = 8
V7X_VMEM_BYTES = 64 * 1024 * 1024
VMEM_LIMIT = V7X_VMEM_BYTES - 8 * 1024 * 1024


def _cp(*sem):
    return pltpu.CompilerParams(dimension_semantics=sem, vmem_limit_bytes=VMEM_LIMIT)


def _dot(a, b):
    return jnp.dot(a, b, preferred_element_type=F32)


def _dot_nt(a, b):
    return lax.dot_general(a, b, (((1,), (1,)), ((), ())), preferred_element_type=F32)


def _rms(x, g):
    return x * lax.rsqrt(jnp.mean(x * x, axis=-1, keepdims=True) + EPS) * g


def _sigmoid(x):
    return 1.0 / (1.0 + jnp.exp(-x))


def _log_sigmoid(x):
    return jnp.minimum(x, 0.0) - jnp.log1p(jnp.exp(-jnp.abs(x)))


def _resident(shape):
    return pl.BlockSpec(shape, lambda *_: (0,) * len(shape), pipeline_mode=pl.Buffered(1))


def _memkv_kernel(x_ref, w_ref, k_ref, v_ref):
    y = _dot(x_ref[...].astype(BF16), w_ref[...].astype(BF16))
    k_ref[...] = y[:, :MIX]
    v_ref[...] = y[:, MIX:]


def _memkv_proj(mem, w_mem_kv):
    depth = w_mem_kv.shape[0]
    b, n_mem, d = mem.shape
    rows = b * n_mem
    out = jax.ShapeDtypeStruct((depth, rows, MIX), F32)
    k, v = pl.pallas_call(
        _memkv_kernel,
        grid=(depth,),
        in_specs=[pl.BlockSpec((rows, d), lambda l: (0, 0)),
                  pl.BlockSpec((None, d, 2 * MIX), lambda l: (l, 0, 0))],
        out_specs=[pl.BlockSpec((None, rows, MIX), lambda l: (l, 0, 0))] * 2,
        out_shape=[out, out],
        compiler_params=_cp("arbitrary"),
        name="memkv_proj",
    )(mem.reshape(rows, d), w_mem_kv)
    return k.reshape(depth, b, n_mem, MIX), v.reshape(depth, b, n_mem, MIX)


def _norm_matmul_kernel(x_ref, g_ref, w_ref, *out_refs):
    xn = _rms(x_ref[...], g_ref[...]).astype(BF16)
    off = 0
    for o_ref in out_refs:
        n = o_ref.shape[-1]
        o_ref[...] = _dot(xn, w_ref[:, off:off + n])
        off += n


def _row_tile(m):
    return min(m, 512)


def _norm_matmul(x, g, w, widths, name):
    m, d = x.shape
    tm = _row_tile(m)
    return pl.pallas_call(
        _norm_matmul_kernel,
        grid=(m // tm,),
        in_specs=[pl.BlockSpec((tm, d), lambda i: (i, 0)),
                  _resident((1, d)),
                  _resident(w.shape)],
        out_specs=[pl.BlockSpec((tm, n), lambda i: (i, 0)) for n in widths],
        out_shape=[jax.ShapeDtypeStruct((m, n), F32) for n in widths],
        compiler_params=_cp("arbitrary"),
        name=name,
    )(x, g.reshape(1, d), w)


def _inproj_a_kernel(x_ref, g_ref, lb_ref, w_ref, o_ref):
    xn = _rms(x_ref[...], g_ref[...]).astype(BF16)
    q = _dot(xn, w_ref[:, 0:MIX])
    o_ref[:, 0:MIX] = q * _sigmoid(q)
    f = _dot(xn, w_ref[:, MIX:2 * MIX])
    lb = lb_ref[...]
    a = jnp.log(lb)
    b = jnp.log1p(-lb) + _log_sigmoid(f)
    o_ref[:, MIX:2 * MIX] = jnp.maximum(a, b) + jnp.log1p(jnp.exp(-jnp.abs(a - b)))
    o_ref[:, 2 * MIX:3 * MIX] = _dot(xn, w_ref[:, 2 * MIX:3 * MIX])
    gt = _dot(xn, w_ref[:, 3 * MIX:4 * MIX])
    o_ref[:, 3 * MIX:4 * MIX] = gt * _sigmoid(gt)
    o_ref[:, 4 * MIX:5 * MIX] = _dot(xn, w_ref[:, 4 * MIX:5 * MIX])


def _inproj_a(x, g, lb, w):
    m, d = x.shape
    tm = _row_tile(m)
    n = w.shape[1]
    return pl.pallas_call(
        _inproj_a_kernel,
        grid=(m // tm,),
        in_specs=[pl.BlockSpec((tm, d), lambda i: (i, 0)),
                  _resident((1, d)),
                  _resident((1, MIX)),
                  _resident(w.shape)],
        out_specs=pl.BlockSpec((tm, n), lambda i: (i, 0)),
        out_shape=jax.ShapeDtypeStruct((m, n), F32),
        compiler_params=_cp("arbitrary"),
        name="inproj_a",
    )(x, g.reshape(1, d), lb.reshape(1, MIX), w)


def _hgrn_consts():
    c = CHUNK
    t = np.arange(c)[:, None]
    j = np.arange(c)[None, :]
    mats = [j <= t]
    levels = []
    for m in (SUB, 2 * SUB, 4 * SUB):
        base = (t // (2 * m)) * (2 * m)
        lower = (t - base) >= m
        mats.append((lower & (j >= base + m) & (j <= t)) | (~lower & (j >= t + 1) & (j <= base + m - 1)))
        levels.append(((t // (2 * m)) == (j // (2 * m))) & ((t % (2 * m)) >= m) & ((j % (2 * m)) < m))
    mats.append(j >= t + 1)
    diags = [(j == t)]
    for d in range(1, SUB):
        mats.append(((t % SUB) >= d) & (j >= t - d + 1) & (j <= t))
        diags.append((j == t - d) & ((t % SUB) >= d))
    cm = np.concatenate(mats, 0).astype(np.float32)
    masks = np.stack(levels + diags).astype(np.float32)
    return jnp.asarray(cm, BF16), jnp.asarray(masks, F32)


def _hgrn_chunk(q, lf, v, st, cm, mk_ref):
    c = CHUNK
    hi = lf.astype(BF16)
    lo = (lf - hi.astype(F32)).astype(BF16)
    x = jnp.exp(_dot(cm, hi) + _dot(cm, lo))
    k = 1.0 - jnp.exp(lf)
    eb = x[0:c]
    o = _dot_nt((q * eb).astype(BF16), st.astype(BF16))
    att = mk_ref[3] * jnp.sum(q * k, axis=1, keepdims=True)
    for li in range(3):
        a = x[c * (1 + li):c * (2 + li)]
        att = att + mk_ref[li] * _dot_nt((q * a).astype(BF16), (k * a).astype(BF16))
    for d in range(1, SUB):
        gd = x[c * (4 + d):c * (5 + d)]
        p = q * pltpu.roll(k, d, 0) * gd
        att = att + mk_ref[3 + d] * jnp.sum(p, axis=1, keepdims=True)
    o = o + _dot(att.astype(BF16), v.astype(BF16))
    kh = (k * x[4 * c:5 * c]).astype(BF16)
    st_new = st * eb[c - 1:c, :] + _dot(v.T.astype(BF16), kh)
    return o, st_new


def _hgrn_scan_kernel(q_ref, lf_ref, v_ref, g_ref, gain_ref, cm_ref, mk_ref, on_ref, s_ref, st_ref, *, n_chunks):
    l = pl.program_id(1)

    @pl.when(l == 0)
    def _():
        st_ref[...] = jnp.zeros_like(st_ref)

    cm = cm_ref[...]
    gain = gain_ref[...]
    sts = [st_ref[hd] for hd in range(HEADS)]
    for ci in range(n_chunks):
        rows = slice(ci * CHUNK, (ci + 1) * CHUNK)
        for hd in range(HEADS):
            cols = slice(hd * DH, (hd + 1) * DH)
            o, sts[hd] = _hgrn_chunk(q_ref[rows, cols], lf_ref[rows, cols], v_ref[rows, cols], sts[hd], cm, mk_ref)
            on_ref[rows, cols] = _rms(o, gain) * g_ref[rows, cols]
    for hd in range(HEADS):
        st_ref[hd] = sts[hd]

    @pl.when(l == pl.num_programs(1) - 1)
    def _():
        for hd in range(HEADS):
            s_ref[hd] = sts[hd].T


def _hgrn_scan(y, gain, tokens_per_step=256):
    b, l, _ = y.shape
    t = min(tokens_per_step, l)
    assert l % t == 0 and t % CHUNK == 0
    cm, masks = _hgrn_consts()

    def col(group):
        return pl.BlockSpec((None, t, MIX), lambda bi, li: (bi, li, group))

    return pl.pallas_call(
        functools.partial(_hgrn_scan_kernel, n_chunks=t // CHUNK),
        grid=(b, l // t),
        in_specs=[col(0), col(1), col(2), col(3),
                  _resident((1, DH)), _resident(cm.shape), _resident(masks.shape)],
        out_specs=[pl.BlockSpec((None, t, MIX), lambda bi, li: (bi, li, 0)),
                   pl.BlockSpec((None, HEADS, DH, DH), lambda bi, li: (bi, 0, 0, 0))],
        out_shape=[jax.ShapeDtypeStruct((b, l, MIX), F32),
                   jax.ShapeDtypeStruct((b, HEADS, DH, DH), F32)],
        scratch_shapes=[pltpu.VMEM((HEADS, DH, DH), F32)],
        compiler_params=_cp("arbitrary", "arbitrary"),
        name="hgrn_scan",
    )(y, y, y, y, gain.reshape(1, DH), cm, masks)


def _hgrn_step_kernel(q_ref, lf_ref, v_ref, g_ref, gain_ref, s_ref, on_ref, so_ref, o_scr, *, n_samples):
    lf = lf_ref[...]
    f = jnp.exp(lf)
    ft = f.T
    kt = (1.0 - f).T
    qt = q_ref[...].T
    v = v_ref[...]
    for i in range(n_samples):
        sn = ft[:, i:i + 1] * s_ref[i] + kt[:, i:i + 1] * v[i:i + 1, :]
        so_ref[i] = sn
        o_scr[i:i + 1, :] = jnp.sum(qt[:, i:i + 1] * sn, axis=0, keepdims=True)
    on_ref[...] = _rms(o_scr[...], gain_ref[...]) * g_ref[...]


def _hgrn_step(y, gain, state, layer):
    n = y.shape[0]

    def col(group):
        return pl.BlockSpec((n, DH), lambda hi: (0, group * HEADS + hi))

    return pl.pallas_call(
        functools.partial(_hgrn_step_kernel, n_samples=n),
        grid=(HEADS,),
        in_specs=[col(0), col(1), col(2), col(3), _resident((1, DH)),
                  pl.BlockSpec((None, n, None, DH, DH), lambda hi: (layer, 0, hi, 0, 0))],
        out_specs=[pl.BlockSpec((n, DH), lambda hi: (0, hi)),
                   pl.BlockSpec((n, None, DH, DH), lambda hi: (0, hi, 0, 0))],
        out_shape=[jax.ShapeDtypeStruct((n, MIX), F32),
                   jax.ShapeDtypeStruct(state.shape[1:], F32)],
        scratch_shapes=[pltpu.VMEM((n, DH), F32)],
        compiler_params=_cp("arbitrary"),
        name="hgrn_step",
    )(y, y, y, y, gain.reshape(1, DH), state)


def _mix_out_prompt_kernel(h_ref, a_ref, qm_ref, mk_ref, mv_ref, wo_ref, g_ref, o_ref, *, scale):
    mix = _dot(a_ref[...].astype(BF16), wo_ref[0:MIX, :])
    for hd in range(HEADS):
        sl = slice(hd * DH, (hd + 1) * DH)
        s = _dot_nt((qm_ref[:, sl] * scale).astype(BF16), mk_ref[:, sl].astype(BF16))
        p = jnp.exp(s - jnp.max(s, axis=-1, keepdims=True))
        om = _dot(p.astype(BF16), mv_ref[:, sl].astype(BF16)) / jnp.sum(p, axis=-1, keepdims=True)
        mix = mix + _dot(om.astype(BF16), wo_ref[MIX + hd * DH:MIX + (hd + 1) * DH, :])
    o_ref[...] = h_ref[...] + _rms(mix, g_ref[...])


def _mix_out_prompt(h, a, y, qm_col, mem_k, mem_v, w_o, g):
    b, l, d = h.shape
    tm = _row_tile(l)
    n_mem = mem_k.shape[1]
    tok = lambda w, c=0: pl.BlockSpec((None, tm, w), lambda bi, li: (bi, li, c))
    mem = pl.BlockSpec((None, n_mem, MIX), lambda bi, li: (bi, 0, 0))
    return pl.pallas_call(
        functools.partial(_mix_out_prompt_kernel, scale=DH ** -0.5),
        grid=(b, l // tm),
        in_specs=[tok(d), tok(MIX), tok(MIX, qm_col), mem, mem, _resident(w_o.shape), _resident((1, d))],
        out_specs=tok(d),
        out_shape=jax.ShapeDtypeStruct(h.shape, F32),
        compiler_params=_cp("arbitrary", "arbitrary"),
        name="mix_out_prompt",
    )(h, a, y, mem_k, mem_v, w_o, g.reshape(1, d))


def _head_rows(x_row):
    rows = [x_row[:, hd * DH:(hd + 1) * DH] for hd in range(HEADS)]
    return jnp.concatenate(rows + [jnp.zeros((8 - HEADS, DH), x_row.dtype)], axis=0)


def _own_head(lanes):
    r = lax.broadcasted_iota(jnp.int32, (8, lanes), 0)
    c = lax.broadcasted_iota(jnp.int32, (8, lanes), 1)
    return jnp.bitwise_and(c, HEADS - 1) == r


def _mix_out_sample_kernel(h_ref, a_ref, qm_ref, mk_ref, mv_ref, wo_ref, g_ref, o_ref, om_scr, *, scale, tb):
    own = _own_head(mk_ref.shape[1])
    for i in range(tb):
        q4 = _head_rows(qm_ref[i:i + 1, :] * scale).astype(BF16)
        s = jnp.where(own, _dot_nt(q4, mk_ref[i].astype(BF16)), -1e30)
        p = jnp.exp(s - jnp.max(s, axis=-1, keepdims=True))
        o4 = _dot(p.astype(BF16), mv_ref[i].astype(BF16)) / jnp.sum(p, axis=-1, keepdims=True)
        for hd in range(HEADS):
            om_scr[i:i + 1, hd * DH:(hd + 1) * DH] = o4[hd:hd + 1, :]
    mix = _dot(a_ref[...].astype(BF16), wo_ref[0:MIX, :]) + _dot(om_scr[...].astype(BF16), wo_ref[MIX:, :])
    o_ref[...] = h_ref[...] + _rms(mix, g_ref[...])


def _mix_out_sample(h, a, y, qm_col, mem_k, mem_v, layer, w_o, g):
    n, d = h.shape
    tb = min(n, 8)
    rows = mem_k.shape[2]
    tok = lambda w, c=0: pl.BlockSpec((tb, w), lambda i: (i, c))
    mem = pl.BlockSpec((None, tb, rows, DH), lambda i: (layer, i, 0, 0))
    return pl.pallas_call(
        functools.partial(_mix_out_sample_kernel, scale=DH ** -0.5, tb=tb),
        grid=(n // tb,),
        in_specs=[tok(d), tok(MIX), tok(MIX, qm_col), mem, mem, _resident(w_o.shape), _resident((1, d))],
        out_specs=tok(d),
        out_shape=jax.ShapeDtypeStruct(h.shape, F32),
        scratch_shapes=[pltpu.VMEM((tb, MIX), F32)],
        compiler_params=_cp("arbitrary"),
        name="mix_out_sample",
    )(h, a, y, mem_k, mem_v, w_o, g.reshape(1, d))


def _ffn_kernel(h_ref, g_in_ref, wgu_ref, wd_ref, g_out_ref, o_ref, act_ref, *, d_ff, tc):
    x = h_ref[...]
    xn = _rms(x, g_in_ref[...]).astype(BF16)
    for c0 in range(0, d_ff, tc):
        gate = _dot(xn, wgu_ref[:, c0:c0 + tc])
        up = _dot(xn, wgu_ref[:, d_ff + c0:d_ff + c0 + tc])
        act_ref[:, c0:c0 + tc] = (gate * _sigmoid(gate) * up).astype(BF16)
    y = _dot(act_ref[...], wd_ref[...])
    o_ref[...] = x + _rms(y, g_out_ref[...])


def _ffn(h, g_in, w_gu, w_down, g_out):
    m, d = h.shape
    d_ff = w_down.shape[0]
    tm = _row_tile(m)
    tc = 256
    assert d_ff % tc == 0
    return pl.pallas_call(
        functools.partial(_ffn_kernel, d_ff=d_ff, tc=tc),
        grid=(m // tm,),
        in_specs=[pl.BlockSpec((tm, d), lambda i: (i, 0)),
                  _resident((1, d)), _resident(w_gu.shape), _resident(w_down.shape), _resident((1, d))],
        out_specs=pl.BlockSpec((tm, d), lambda i: (i, 0)),
        out_shape=jax.ShapeDtypeStruct((m, d), F32),
        scratch_shapes=[pltpu.VMEM((tm, d_ff), BF16)],
        compiler_params=_cp("arbitrary"),
        name="ffn",
    )(h, g_in.reshape(1, d), w_gu, w_down, g_out.reshape(1, d))


def _sb_consts(with_totals):
    s = np.arange(SB_BLK)
    u = (s[:, None] > s[None, :]).astype(np.float32)
    if with_totals:
        u = np.concatenate([u, np.ones((SB_BLK, SB_BLK), np.float32)], 1)
    return jnp.asarray(u, BF16)


def _log2_keep(zz):
    return jnp.minimum(zz, 0.0) - jnp.log2(1.0 + jnp.exp2(-jnp.abs(zz)))


def _sb_prompt_kernel(bias_ref, q_ref, k_ref, v_ref, u_ref, o_ref, *, scale, tq):
    hd = pl.program_id(1)
    i = pl.program_id(2)
    nsub = tq // SB_BLK
    nbias = bias_ref[hd] * (-LOG2E)
    qn = (q_ref[...] * (-scale * LOG2E)).astype(BF16)
    u = u_ref[...]
    row = lax.broadcasted_iota(jnp.int32, (tq, SB_BLK), 0)
    col = lax.broadcasted_iota(jnp.int32, (tq, SB_BLK), 1)

    def step(j, acc, c, mask):
        off = pl.multiple_of(j * SB_BLK, SB_BLK)
        kj = k_ref[pl.ds(off, SB_BLK), :].astype(BF16)
        vj = v_ref[pl.ds(off, SB_BLK), :].astype(BF16)
        zz = _dot_nt(qn, kj) + nbias
        lk = _log2_keep(zz)
        if mask is not None:
            lk = jnp.where(mask, lk, 0.0)
        w = jnp.exp2(lk - zz + c + _dot(lk.astype(BF16), u))
        if mask is not None:
            w = jnp.where(mask, w, 0.0)
        return acc + _dot(w.astype(BF16), vj), c + jnp.sum(lk, axis=1, keepdims=True)

    acc = jnp.zeros((tq, DH), F32)
    c = jnp.zeros((tq, 1), F32)
    for jj in reversed(range(nsub)):
        acc, c = step(i * nsub + jj, acc, c, col + jj * SB_BLK < row)
    def body(it, carry):
        for s in range(SB_UNROLL):
            carry = step(i * nsub - 1 - it * SB_UNROLL - s, carry[0], carry[1], None)
        return carry

    acc, c = lax.fori_loop(0, i * (nsub // SB_UNROLL), body, (acc, c))
    o_ref[...] = acc


def _sb_prompt(yb, k, v, bias):
    b, l, _ = yb.shape
    tq = min(SB_QTILE, l)
    assert l % tq == 0 and tq % SB_BLK == 0
    u = _sb_consts(False)
    kv = pl.BlockSpec((None, l, DH), lambda bi, hi, qi: (bi, 0, hi))
    qo = pl.BlockSpec((None, tq, DH), lambda bi, hi, qi: (bi, qi, hi))
    return pl.pallas_call(
        functools.partial(_sb_prompt_kernel, scale=DH ** -0.5, tq=tq),
        grid=(b, HEADS, l // tq),
        in_specs=[pl.BlockSpec(memory_space=pltpu.SMEM), qo, kv, kv, _resident(u.shape)],
        out_specs=qo,
        out_shape=jax.ShapeDtypeStruct((b, l, MIX), F32),
        compiler_params=_cp("arbitrary", "arbitrary", "arbitrary"),
        name="sb_prompt",
    )(bias, yb, k, v, u)


def _sb_sample_kernel(pt_ref, bias_ref, q_ref, *refs, n_pages, scale):
    k_refs = refs[:n_pages]
    v_refs = refs[n_pages:2 * n_pages]
    u_ref, o_ref = refs[2 * n_pages:]
    lanes = k_refs[0].shape[0]
    per_page = lanes // SB_BLK
    q4 = _head_rows(q_ref[...] * (-scale * LOG2E)).astype(BF16)
    own = _own_head(lanes)
    r = lax.broadcasted_iota(jnp.int32, (8, lanes), 0)
    nbias = jnp.zeros((8, lanes), F32)
    for hd in range(HEADS):
        nbias = jnp.where(r == hd, bias_ref[hd] * (-LOG2E), nbias)
    zzs, lks, blocks = [], [], []
    for j in range(n_pages):
        zz = _dot_nt(q4, k_refs[j][...].astype(BF16)) + nbias
        lk = jnp.where(own, _log2_keep(zz), 0.0)
        zzs.append(zz)
        lks.append(lk)
        blocks += [lk[:, cc * SB_BLK:(cc + 1) * SB_BLK] for cc in range(per_page)]
    stacked = jnp.concatenate(blocks, axis=0)
    hi = stacked.astype(BF16)
    lo = (stacked - hi.astype(F32)).astype(BF16)
    cs = _dot(hi, u_ref[...]) + _dot(lo, u_ref[...])
    c = jnp.zeros((8, SB_BLK), F32)
    tails = [None] * len(blocks)
    for gi in reversed(range(len(blocks))):
        blk = cs[8 * gi:8 * gi + 8]
        tails[gi] = c + blk[:, :SB_BLK]
        c = c + blk[:, SB_BLK:]
    acc = jnp.zeros((8, DH), F32)
    for j in range(n_pages):
        tail = jnp.concatenate(tails[j * per_page:(j + 1) * per_page], axis=1)
        w = jnp.where(own, jnp.exp2(lks[j] - zzs[j] + tail), 0.0)
        acc = acc + _dot(w.astype(BF16), v_refs[j][...].astype(BF16))
    for hd in range(HEADS):
        o_ref[:, hd * DH:(hd + 1) * DH] = acc[hd:hd + 1, :]


def _sb_sample(yb, cache_k, cache_v, page_table, bias):
    n = yb.shape[0]
    n_pool, page = cache_k.shape[:2]
    n_pages = page_table.shape[1]
    rows = page * HEADS
    assert rows % SB_BLK == 0
    u = _sb_consts(True)
    ck = cache_k.reshape(n_pool, rows, DH)
    cv = cache_v.reshape(n_pool, rows, DH)
    pages = [pl.BlockSpec((None, rows, DH), lambda i, pt, j=j: (pt[i, j], 0, 0)) for j in range(n_pages)]
    row = pl.BlockSpec((None, 1, MIX), lambda i, pt: (i, 0, 0))
    out = pl.pallas_call(
        functools.partial(_sb_sample_kernel, n_pages=n_pages, scale=DH ** -0.5),
        grid_spec=pltpu.PrefetchScalarGridSpec(
            num_scalar_prefetch=1,
            grid=(n,),
            in_specs=[pl.BlockSpec(memory_space=pltpu.SMEM), row] + pages + pages
                     + [pl.BlockSpec(u.shape, lambda i, pt: (0, 0))],
            out_specs=row),
        out_shape=jax.ShapeDtypeStruct((n, 1, MIX), F32),
        compiler_params=_cp("arbitrary"),
        name="sb_sample",
    )(page_table, bias, yb.reshape(n, 1, 2 * MIX), *([ck] * n_pages), *([cv] * n_pages), u)
    return out.reshape(n, MIX)


def _lower_bounds(lb_param):
    c = jnp.cumsum(jax.nn.softmax(lb_param.astype(F32), axis=0), axis=0)
    return c - c[0:1]


def _trunk(x, mem_k, mem_v, state, cache_k, cache_v, page_table, wts):
    g_norm, w_in_a, lbs, hg_norm, w_in_b, sb_bias, g_kv, w_kv, w_o, w_gu, w_down = wts
    b, l, d = x.shape
    sample = state is not None
    m = b * l
    depth = g_norm.shape[0]
    n_a = w_in_a.shape[0]
    h = x.reshape(m, d)
    states = []
    sb_k = sb_v = None
    for layer in range(depth):
        g = g_norm[layer]
        if layer < n_a:
            y = _inproj_a(h, g[0], lbs[layer], w_in_a[layer])
            if sample:
                a, s = _hgrn_step(y, hg_norm[layer], state, layer)
            else:
                a, s = _hgrn_scan(y.reshape(b, l, -1), hg_norm[layer])
            states.append(s)
            qm_col = 4
        else:
            (y,) = _norm_matmul(h, g[0], w_in_b[layer - n_a], (2 * MIX,), "inproj_b")
            if sample:
                a = _sb_sample(y, cache_k, cache_v, page_table, sb_bias[layer - n_a])
            else:
                a = _sb_prompt(y.reshape(b, l, -1), sb_k.reshape(b, l, MIX), sb_v.reshape(b, l, MIX),
                               sb_bias[layer - n_a])
            qm_col = 1
        if sample:
            h = _mix_out_sample(h, a.reshape(m, MIX), y, qm_col, mem_k, mem_v, layer, w_o[layer], g[1])
        else:
            h = _mix_out_prompt(h.reshape(b, l, d), a.reshape(b, l, MIX), y.reshape(b, l, -1), qm_col,
                                mem_k[layer], mem_v[layer], w_o[layer], g[1]).reshape(m, d)
        h = _ffn(h, g[2], w_gu[layer], w_down[layer], g[3])
        if layer == n_a - 1:
            sb_k, sb_v = _norm_matmul(h, g_kv, w_kv, (MIX, MIX), "kv_proj")
    return (h.reshape(b, l, d), jnp.stack(states),
            sb_k.reshape(b, l, HEADS, DH), sb_v.reshape(b, l, HEADS, DH))


def kernel(x_prompt, x_sample, mem_prompt, cache_sb_k, cache_sb_v, cache_mem_k, cache_mem_v, state_hgrn,
           page_table, g_norm, w_in_a, hg_lb, hg_norm, w_in_b, sb_bias, g_kv, w_kv, w_mem_kv, w_o, w_gu, w_down):
    depth = g_norm.shape[0]
    wts = (g_norm, w_in_a.astype(BF16), _lower_bounds(hg_lb), hg_norm, w_in_b.astype(BF16), sb_bias, g_kv,
           w_kv.astype(BF16), w_o.astype(BF16), w_gu.astype(BF16), w_down.astype(BF16))
    bp, n_mem = mem_prompt.shape[:2]
    mk_p, mv_p = _memkv_proj(mem_prompt, w_mem_kv)
    y_p, s_p, k_p, v_p = _trunk(x_prompt, mk_p, mv_p, None, None, None, None, wts)
    ds = x_sample.shape[0]
    cmk = cache_mem_k.reshape(depth, ds, n_mem * HEADS, DH)
    cmv = cache_mem_v.reshape(depth, ds, n_mem * HEADS, DH)
    y_s, s_s, k_s, v_s = _trunk(x_sample, cmk, cmv, state_hgrn, cache_sb_k, cache_sb_v, page_table, wts)
    return (y_p, y_s, s_p, s_s.astype(state_hgrn.dtype), k_p, v_p, k_s, v_s,
            mk_p.reshape(depth, bp, n_mem, HEADS, DH), mv_p.reshape(depth, bp, n_mem, HEADS, DH))
```

```python
import functools

import numpy as np
import jax
import jax.numpy as jnp
from jax import lax
from jax.experimental import pallas as pl
from jax.experimental.pallas import tpu as pltpu

F32 = jnp.float32
BF16 = jnp.bfloat16
EPS = 1e-6
LOG2E = 1.4426950408889634
HEADS = 4
DH = 128
MIX = HEADS * DH
CHUNK = 64
SUB = 8
SB_BLK = 128
SB_QTILE = 1024
SB_UNROLL = 8
V7X_VMEM_BYTES = 64 * 1024 * 1024
VMEM_LIMIT = V7X_VMEM_BYTES - 8 * 1024 * 1024


def _cp(*sem):
    return pltpu.CompilerParams(dimension_semantics=sem, vmem_limit_bytes=VMEM_LIMIT)


def _dot(a, b):
    return jnp.dot(a, b, preferred_element_type=F32)


def _dot_nt(a, b):
    return lax.dot_general(a, b, (((1,), (1,)), ((), ())), preferred_element_type=F32)


def _rms(x, g):
    return x * lax.rsqrt(jnp.mean(x * x, axis=-1, keepdims=True) + EPS) * g


def _sigmoid(x):
    return 1.0 / (1.0 + jnp.exp(-x))


def _log_sigmoid(x):
    return jnp.minimum(x, 0.0) - jnp.log1p(jnp.exp(-jnp.abs(x)))


def _resident(shape):
    return pl.BlockSpec(shape, lambda *_: (0,) * len(shape), pipeline_mode=pl.Buffered(1))


def _memkv_kernel(x_ref, w_ref, k_ref, v_ref):
    y = _dot(x_ref[...].astype(BF16), w_ref[...].astype(BF16))
    k_ref[...] = y[:, :MIX]
    v_ref[...] = y[:, MIX:]


def _memkv_proj(mem, w_mem_kv):
    depth = w_mem_kv.shape[0]
    b, n_mem, d = mem.shape
    rows = b * n_mem
    out = jax.ShapeDtypeStruct((depth, rows, MIX), F32)
    k, v = pl.pallas_call(
        _memkv_kernel,
        grid=(depth,),
        in_specs=[pl.BlockSpec((rows, d), lambda l: (0, 0)),
                  pl.BlockSpec((None, d, 2 * MIX), lambda l: (l, 0, 0))],
        out_specs=[pl.BlockSpec((None, rows, MIX), lambda l: (l, 0, 0))] * 2,
        out_shape=[out, out],
        compiler_params=_cp("arbitrary"),
        name="memkv_proj",
    )(mem.reshape(rows, d), w_mem_kv)
    return k.reshape(depth, b, n_mem, MIX), v.reshape(depth, b, n_mem, MIX)


def _norm_matmul_kernel(x_ref, g_ref, w_ref, *out_refs):
    xn = _rms(x_ref[...], g_ref[...]).astype(BF16)
    off = 0
    for o_ref in out_refs:
        n = o_ref.shape[-1]
        o_ref[...] = _dot(xn, w_ref[:, off:off + n])
        off += n


def _row_tile(m):
    return min(m, 512)


def _norm_matmul(x, g, w, widths, name):
    m, d = x.shape
    tm = _row_tile(m)
    return pl.pallas_call(
        _norm_matmul_kernel,
        grid=(m // tm,),
        in_specs=[pl.BlockSpec((tm, d), lambda i: (i, 0)),
                  _resident((1, d)),
                  _resident(w.shape)],
        out_specs=[pl.BlockSpec((tm, n), lambda i: (i, 0)) for n in widths],
        out_shape=[jax.ShapeDtypeStruct((m, n), F32) for n in widths],
        compiler_params=_cp("arbitrary"),
        name=name,
    )(x, g.reshape(1, d), w)


def _inproj_a_kernel(x_ref, g_ref, lb_ref, w_ref, o_ref):
    xn = _rms(x_ref[...], g_ref[...]).astype(BF16)
    q = _dot(xn, w_ref[:, 0:MIX])
    o_ref[:, 0:MIX] = q * _sigmoid(q)
    f = _dot(xn, w_ref[:, MIX:2 * MIX])
    lb = lb_ref[...]
    a = jnp.log(lb)
    b = jnp.log1p(-lb) + _log_sigmoid(f)
    o_ref[:, MIX:2 * MIX] = jnp.maximum(a, b) + jnp.log1p(jnp.exp(-jnp.abs(a - b)))
    o_ref[:, 2 * MIX:3 * MIX] = _dot(xn, w_ref[:, 2 * MIX:3 * MIX])
    gt = _dot(xn, w_ref[:, 3 * MIX:4 * MIX])
    o_ref[:, 3 * MIX:4 * MIX] = gt * _sigmoid(gt)
    o_ref[:, 4 * MIX:5 * MIX] = _dot(xn, w_ref[:, 4 * MIX:5 * MIX])


def _inproj_a(x, g, lb, w):
    m, d = x.shape
    tm = _row_tile(m)
    n = w.shape[1]
    return pl.pallas_call(
        _inproj_a_kernel,
        grid=(m // tm,),
        in_specs=[pl.BlockSpec((tm, d), lambda i: (i, 0)),
                  _resident((1, d)),
                  _resident((1, MIX)),
                  _resident(w.shape)],
        out_specs=pl.BlockSpec((tm, n), lambda i: (i, 0)),
        out_shape=jax.ShapeDtypeStruct((m, n), F32),
        compiler_params=_cp("arbitrary"),
        name="inproj_a",
    )(x, g.reshape(1, d), lb.reshape(1, MIX), w)


def _hgrn_consts():
    c = CHUNK
    t = np.arange(c)[:, None]
    j = np.arange(c)[None, :]
    mats = [j <= t]
    levels = []
    for m in (SUB, 2 * SUB, 4 * SUB):
        base = (t // (2 * m)) * (2 * m)
        lower = (t - base) >= m
        mats.append((lower & (j >= base + m) & (j <= t)) | (~lower & (j >= t + 1) & (j <= base + m - 1)))
        levels.append(((t // (2 * m)) == (j // (2 * m))) & ((t % (2 * m)) >= m) & ((j % (2 * m)) < m))
    mats.append(j >= t + 1)
    diags = [(j == t)]
    for d in range(1, SUB):
        mats.append(((t % SUB) >= d) & (j >= t - d + 1) & (j <= t))
        diags.append((j == t - d) & ((t % SUB) >= d))
    cm = np.concatenate(mats, 0).astype(np.float32)
    masks = np.stack(levels + diags).astype(np.float32)
    return jnp.asarray(cm, BF16), jnp.asarray(masks, F32)


def _hgrn_decays(lf, cm):
    hi = lf.astype(BF16)
    lo = (lf - hi.astype(F32)).astype(BF16)
    return jnp.exp(_dot(cm, hi) + _dot(cm, lo)), 1.0 - jnp.exp(lf)


def _hgrn_intra(q, k, x, mk_ref):
    c = CHUNK
    att = mk_ref[3] * jnp.sum(q * k, axis=1, keepdims=True)
    for li in range(3):
        a = x[c * (1 + li):c * (2 + li)]
        att = att + mk_ref[li] * _dot_nt((q * a).astype(BF16), (k * a).astype(BF16))
    for d in range(1, SUB):
        gd = x[c * (4 + d):c * (5 + d)]
        p = q * pltpu.roll(k, d, 0) * gd
        att = att + mk_ref[3 + d] * jnp.sum(p, axis=1, keepdims=True)
    return att


def _hgrn_state(q, k, v, x, att, st):
    c = CHUNK
    eb = x[0:c]
    o = _dot_nt((q * eb).astype(BF16), st.astype(BF16)) + _dot(att.astype(BF16), v.astype(BF16))
    kh = (k * x[4 * c:5 * c]).astype(BF16)
    return o, st * eb[c - 1:c, :] + _dot(v.T.astype(BF16), kh)


def _hgrn_scan_kernel(q_ref, lf_ref, v_ref, g_ref, gain_ref, cm_ref, mk_ref, on_ref, s_ref, st_ref, *, n_chunks):
    l = pl.program_id(1)

    @pl.when(l == 0)
    def _():
        st_ref[...] = jnp.zeros_like(st_ref)

    cm = cm_ref[...]
    gain = gain_ref[...]
    sts = [st_ref[hd] for hd in range(HEADS)]
    heads = [slice(hd * DH, (hd + 1) * DH) for hd in range(HEADS)]
    for ci in range(n_chunks):
        rows = slice(ci * CHUNK, (ci + 1) * CHUNK)
        qs = [q_ref[rows, cols] for cols in heads]
        xk = [_hgrn_decays(lf_ref[rows, cols], cm) for cols in heads]
        atts = [_hgrn_intra(qs[hd], xk[hd][1], xk[hd][0], mk_ref) for hd in range(HEADS)]
        for hd, cols in enumerate(heads):
            o, sts[hd] = _hgrn_state(qs[hd], xk[hd][1], v_ref[rows, cols], xk[hd][0], atts[hd], sts[hd])
            on_ref[rows, cols] = _rms(o, gain) * g_ref[rows, cols]
    for hd in range(HEADS):
        st_ref[hd] = sts[hd]

    @pl.when(l == pl.num_programs(1) - 1)
    def _():
        for hd in range(HEADS):
            s_ref[hd] = sts[hd].T


def _hgrn_scan(y, gain, tokens_per_step=256):
    b, l, _ = y.shape
    t = min(tokens_per_step, l)
    assert l % t == 0 and t % CHUNK == 0
    cm, masks = _hgrn_consts()

    def col(group):
        return pl.BlockSpec((None, t, MIX), lambda bi, li: (bi, li, group))

    return pl.pallas_call(
        functools.partial(_hgrn_scan_kernel, n_chunks=t // CHUNK),
        grid=(b, l // t),
        in_specs=[col(0), col(1), col(2), col(3),
                  _resident((1, DH)), _resident(cm.shape), _resident(masks.shape)],
        out_specs=[pl.BlockSpec((None, t, MIX), lambda bi, li: (bi, li, 0)),
                   pl.BlockSpec((None, HEADS, DH, DH), lambda bi, li: (bi, 0, 0, 0))],
        out_shape=[jax.ShapeDtypeStruct((b, l, MIX), F32),
                   jax.ShapeDtypeStruct((b, HEADS, DH, DH), F32)],
        scratch_shapes=[pltpu.VMEM((HEADS, DH, DH), F32)],
        compiler_params=_cp("arbitrary", "arbitrary"),
        name="hgrn_scan",
    )(y, y, y, y, gain.reshape(1, DH), cm, masks)


def _hgrn_step_kernel(q_ref, lf_ref, v_ref, g_ref, gain_ref, s_ref, on_ref, so_ref, o_scr, *, n_samples):
    lf = lf_ref[...]
    f = jnp.exp(lf)
    ft = f.T
    kt = (1.0 - f).T
    qt = q_ref[...].T
    v = v_ref[...]
    for i in range(n_samples):
        sn = ft[:, i:i + 1] * s_ref[i] + kt[:, i:i + 1] * v[i:i + 1, :]
        so_ref[i] = sn
        o_scr[i:i + 1, :] = jnp.sum(qt[:, i:i + 1] * sn, axis=0, keepdims=True)
    on_ref[...] = _rms(o_scr[...], gain_ref[...]) * g_ref[...]


def _hgrn_step(y, gain, state, layer):
    n = y.shape[0]

    def col(group):
        return pl.BlockSpec((n, DH), lambda hi: (0, group * HEADS + hi))

    return pl.pallas_call(
        functools.partial(_hgrn_step_kernel, n_samples=n),
        grid=(HEADS,),
        in_specs=[col(0), col(1), col(2), col(3), _resident((1, DH)),
                  pl.BlockSpec((None, n, None, DH, DH), lambda hi: (layer, 0, hi, 0, 0))],
        out_specs=[pl.BlockSpec((n, DH), lambda hi: (0, hi)),
                   pl.BlockSpec((n, None, DH, DH), lambda hi: (0, hi, 0, 0))],
        out_shape=[jax.ShapeDtypeStruct((n, MIX), F32),
                   jax.ShapeDtypeStruct(state.shape[1:], F32)],
        scratch_shapes=[pltpu.VMEM((n, DH), F32)],
        compiler_params=_cp("arbitrary"),
        name="hgrn_step",
    )(y, y, y, y, gain.reshape(1, DH), state)


def _mix_out_prompt_kernel(h_ref, a_ref, qm_ref, mk_ref, mv_ref, wo_ref, g_ref, o_ref, *, scale):
    mix = _dot(a_ref[...].astype(BF16), wo_ref[0:MIX, :])
    for hd in range(HEADS):
        sl = slice(hd * DH, (hd + 1) * DH)
        s = _dot_nt((qm_ref[:, sl] * scale).astype(BF16), mk_ref[:, sl].astype(BF16))
        p = jnp.exp(s - jnp.max(s, axis=-1, keepdims=True))
        om = _dot(p.astype(BF16), mv_ref[:, sl].astype(BF16)) / jnp.sum(p, axis=-1, keepdims=True)
        mix = mix + _dot(om.astype(BF16), wo_ref[MIX + hd * DH:MIX + (hd + 1) * DH, :])
    o_ref[...] = h_ref[...] + _rms(mix, g_ref[...])


def _mix_out_prompt(h, a, y, qm_col, mem_k, mem_v, w_o, g):
    b, l, d = h.shape
    tm = _row_tile(l)
    n_mem = mem_k.shape[1]
    tok = lambda w, c=0: pl.BlockSpec((None, tm, w), lambda bi, li: (bi, li, c))
    mem = pl.BlockSpec((None, n_mem, MIX), lambda bi, li: (bi, 0, 0))
    return pl.pallas_call(
        functools.partial(_mix_out_prompt_kernel, scale=DH ** -0.5),
        grid=(b, l // tm),
        in_specs=[tok(d), tok(MIX), tok(MIX, qm_col), mem, mem, _resident(w_o.shape), _resident((1, d))],
        out_specs=tok(d),
        out_shape=jax.ShapeDtypeStruct(h.shape, F32),
        compiler_params=_cp("arbitrary", "arbitrary"),
        name="mix_out_prompt",
    )(h, a, y, mem_k, mem_v, w_o, g.reshape(1, d))


def _head_rows(x_row):
    rows = [x_row[:, hd * DH:(hd + 1) * DH] for hd in range(HEADS)]
    return jnp.concatenate(rows + [jnp.zeros((8 - HEADS, DH), x_row.dtype)], axis=0)


def _own_head(lanes):
    r = lax.broadcasted_iota(jnp.int32, (8, lanes), 0)
    c = lax.broadcasted_iota(jnp.int32, (8, lanes), 1)
    return jnp.bitwise_and(c, HEADS - 1) == r


def _mix_out_sample_kernel(h_ref, a_ref, qm_ref, mk_ref, mv_ref, wo_ref, g_ref, o_ref, om_scr, *, scale, tb):
    own = _own_head(mk_ref.shape[1])
    for i in range(tb):
        q4 = _head_rows(qm_ref[i:i + 1, :] * scale).astype(BF16)
        s = jnp.where(own, _dot_nt(q4, mk_ref[i].astype(BF16)), -1e30)
        p = jnp.exp(s - jnp.max(s, axis=-1, keepdims=True))
        o4 = _dot(p.astype(BF16), mv_ref[i].astype(BF16)) / jnp.sum(p, axis=-1, keepdims=True)
        for hd in range(HEADS):
            om_scr[i:i + 1, hd * DH:(hd + 1) * DH] = o4[hd:hd + 1, :]
    mix = _dot(a_ref[...].astype(BF16), wo_ref[0:MIX, :]) + _dot(om_scr[...].astype(BF16), wo_ref[MIX:, :])
    o_ref[...] = h_ref[...] + _rms(mix, g_ref[...])


def _mix_out_sample(h, a, y, qm_col, mem_k, mem_v, layer, w_o, g):
    n, d = h.shape
    tb = min(n, 8)
    rows = mem_k.shape[2]
    tok = lambda w, c=0: pl.BlockSpec((tb, w), lambda i: (i, c))
    mem = pl.BlockSpec((None, tb, rows, DH), lambda i: (layer, i, 0, 0))
    return pl.pallas_call(
        functools.partial(_mix_out_sample_kernel, scale=DH ** -0.5, tb=tb),
        grid=(n // tb,),
        in_specs=[tok(d), tok(MIX), tok(MIX, qm_col), mem, mem, _resident(w_o.shape), _resident((1, d))],
        out_specs=tok(d),
        out_shape=jax.ShapeDtypeStruct(h.shape, F32),
        scratch_shapes=[pltpu.VMEM((tb, MIX), F32)],
        compiler_params=_cp("arbitrary"),
        name="mix_out_sample",
    )(h, a, y, mem_k, mem_v, w_o, g.reshape(1, d))


def _ffn_kernel(h_ref, g_in_ref, wgu_ref, wd_ref, g_out_ref, o_ref, act_ref, *, d_ff, tc):
    x = h_ref[...]
    xn = _rms(x, g_in_ref[...]).astype(BF16)
    for c0 in range(0, d_ff, tc):
        gate = _dot(xn, wgu_ref[:, c0:c0 + tc])
        up = _dot(xn, wgu_ref[:, d_ff + c0:d_ff + c0 + tc])
        act_ref[:, c0:c0 + tc] = (gate * _sigmoid(gate) * up).astype(BF16)
    y = _dot(act_ref[...], wd_ref[...])
    o_ref[...] = x + _rms(y, g_out_ref[...])


def _ffn(h, g_in, w_gu, w_down, g_out):
    m, d = h.shape
    d_ff = w_down.shape[0]
    tm = _row_tile(m)
    tc = 256
    assert d_ff % tc == 0
    return pl.pallas_call(
        functools.partial(_ffn_kernel, d_ff=d_ff, tc=tc),
        grid=(m // tm,),
        in_specs=[pl.BlockSpec((tm, d), lambda i: (i, 0)),
                  _resident((1, d)), _resident(w_gu.shape), _resident(w_down.shape), _resident((1, d))],
        out_specs=pl.BlockSpec((tm, d), lambda i: (i, 0)),
        out_shape=jax.ShapeDtypeStruct((m, d), F32),
        scratch_shapes=[pltpu.VMEM((tm, d_ff), BF16)],
        compiler_params=_cp("arbitrary"),
        name="ffn",
    )(h, g_in.reshape(1, d), w_gu, w_down, g_out.reshape(1, d))


def _sb_consts(with_totals):
    s = np.arange(SB_BLK)
    u = (s[:, None] > s[None, :]).astype(np.float32)
    if with_totals:
        u = np.concatenate([u, np.ones((SB_BLK, SB_BLK), np.float32)], 1)
    return jnp.asarray(u, BF16)


def _log2_keep(zz):
    return jnp.minimum(zz, 0.0) - jnp.log2(1.0 + jnp.exp2(-jnp.abs(zz)))


def _sb_prompt_kernel(bias_ref, q_ref, k_ref, v_ref, u_ref, o_ref, *, scale, tq):
    hd = pl.program_id(1)
    i = pl.program_id(2)
    nsub = tq // SB_BLK
    nbias = bias_ref[hd] * (-LOG2E)
    qn = (q_ref[...] * (-scale * LOG2E)).astype(BF16)
    u = u_ref[...]
    row = lax.broadcasted_iota(jnp.int32, (tq, SB_BLK), 0)
    col = lax.broadcasted_iota(jnp.int32, (tq, SB_BLK), 1)

    def step(j, acc, c, mask):
        off = pl.multiple_of(j * SB_BLK, SB_BLK)
        kj = k_ref[pl.ds(off, SB_BLK), :].astype(BF16)
        vj = v_ref[pl.ds(off, SB_BLK), :].astype(BF16)
        zz = _dot_nt(qn, kj) + nbias
        lk = _log2_keep(zz)
        if mask is not None:
            lk = jnp.where(mask, lk, 0.0)
        w = jnp.exp2(lk - zz + c + _dot(lk.astype(BF16), u))
        if mask is not None:
            w = jnp.where(mask, w, 0.0)
        return acc + _dot(w.astype(BF16), vj), c + jnp.sum(lk, axis=1, keepdims=True)

    acc = jnp.zeros((tq, DH), F32)
    c = jnp.zeros((tq, 1), F32)
    for jj in reversed(range(nsub)):
        acc, c = step(i * nsub + jj, acc, c, col + jj * SB_BLK < row)
    def body(it, carry):
        for s in range(SB_UNROLL):
            carry = step(i * nsub - 1 - it * SB_UNROLL - s, carry[0], carry[1], None)
        return carry

    acc, c = lax.fori_loop(0, i * (nsub // SB_UNROLL), body, (acc, c))
    o_ref[...] = acc


def _sb_prompt(yb, k, v, bias):
    b, l, _ = yb.shape
    tq = min(SB_QTILE, l)
    assert l % tq == 0 and tq % SB_BLK == 0
    u = _sb_consts(False)
    kv = pl.BlockSpec((None, l, DH), lambda bi, hi, qi: (bi, 0, hi))
    qo = pl.BlockSpec((None, tq, DH), lambda bi, hi, qi: (bi, qi, hi))
    return pl.pallas_call(
        functools.partial(_sb_prompt_kernel, scale=DH ** -0.5, tq=tq),
        grid=(b, HEADS, l // tq),
        in_specs=[pl.BlockSpec(memory_space=pltpu.SMEM), qo, kv, kv, _resident(u.shape)],
        out_specs=qo,
        out_shape=jax.ShapeDtypeStruct((b, l, MIX), F32),
        compiler_params=_cp("arbitrary", "arbitrary", "arbitrary"),
        name="sb_prompt",
    )(bias, yb, k, v, u)


def _sb_sample_kernel(pt_ref, bias_ref, q_ref, *refs, n_pages, scale):
    k_refs = refs[:n_pages]
    v_refs = refs[n_pages:2 * n_pages]
    u_ref, o_ref = refs[2 * n_pages:]
    lanes = k_refs[0].shape[0]
    per_page = lanes // SB_BLK
    q4 = _head_rows(q_ref[...] * (-scale * LOG2E)).astype(BF16)
    own = _own_head(lanes)
    r = lax.broadcasted_iota(jnp.int32, (8, lanes), 0)
    nbias = jnp.zeros((8, lanes), F32)
    for hd in range(HEADS):
        nbias = jnp.where(r == hd, bias_ref[hd] * (-LOG2E), nbias)
    zzs, lks, blocks = [], [], []
    for j in range(n_pages):
        zz = _dot_nt(q4, k_refs[j][...].astype(BF16)) + nbias
        lk = jnp.where(own, _log2_keep(zz), 0.0)
        zzs.append(zz)
        lks.append(lk)
        blocks += [lk[:, cc * SB_BLK:(cc + 1) * SB_BLK] for cc in range(per_page)]
    stacked = jnp.concatenate(blocks, axis=0)
    hi = stacked.astype(BF16)
    lo = (stacked - hi.astype(F32)).astype(BF16)
    cs = _dot(hi, u_ref[...]) + _dot(lo, u_ref[...])
    c = jnp.zeros((8, SB_BLK), F32)
    tails = [None] * len(blocks)
    for gi in reversed(range(len(blocks))):
        blk = cs[8 * gi:8 * gi + 8]
        tails[gi] = c + blk[:, :SB_BLK]
        c = c + blk[:, SB_BLK:]
    acc = jnp.zeros((8, DH), F32)
    for j in range(n_pages):
        tail = jnp.concatenate(tails[j * per_page:(j + 1) * per_page], axis=1)
        w = jnp.where(own, jnp.exp2(lks[j] - zzs[j] + tail), 0.0)
        acc = acc + _dot(w.astype(BF16), v_refs[j][...].astype(BF16))
    for hd in range(HEADS):
        o_ref[:, hd * DH:(hd + 1) * DH] = acc[hd:hd + 1, :]


def _sb_sample(yb, cache_k, cache_v, page_table, bias):
    n = yb.shape[0]
    n_pool, page = cache_k.shape[:2]
    n_pages = page_table.shape[1]
    rows = page * HEADS
    assert rows % SB_BLK == 0
    u = _sb_consts(True)
    ck = cache_k.reshape(n_pool, rows, DH)
    cv = cache_v.reshape(n_pool, rows, DH)
    pages = [pl.BlockSpec((None, rows, DH), lambda i, pt, j=j: (pt[i, j], 0, 0)) for j in range(n_pages)]
    row = pl.BlockSpec((None, 1, MIX), lambda i, pt: (i, 0, 0))
    out = pl.pallas_call(
        functools.partial(_sb_sample_kernel, n_pages=n_pages, scale=DH ** -0.5),
        grid_spec=pltpu.PrefetchScalarGridSpec(
            num_scalar_prefetch=1,
            grid=(n,),
            in_specs=[pl.BlockSpec(memory_space=pltpu.SMEM), row] + pages + pages
                     + [pl.BlockSpec(u.shape, lambda i, pt: (0, 0))],
            out_specs=row),
        out_shape=jax.ShapeDtypeStruct((n, 1, MIX), F32),
        compiler_params=_cp("arbitrary"),
        name="sb_sample",
    )(page_table, bias, yb.reshape(n, 1, 2 * MIX), *([ck] * n_pages), *([cv] * n_pages), u)
    return out.reshape(n, MIX)


def _lower_bounds(lb_param):
    c = jnp.cumsum(jax.nn.softmax(lb_param.astype(F32), axis=0), axis=0)
    return c - c[0:1]


def _trunk(x, mem_k, mem_v, state, cache_k, cache_v, page_table, wts):
    g_norm, w_in_a, lbs, hg_norm, w_in_b, sb_bias, g_kv, w_kv, w_o, w_gu, w_down = wts
    b, l, d = x.shape
    sample = state is not None
    m = b * l
    depth = g_norm.shape[0]
    n_a = w_in_a.shape[0]
    h = x.reshape(m, d)
    states = []
    sb_k = sb_v = None
    for layer in range(depth):
        g = g_norm[layer]
        if layer < n_a:
            y = _inproj_a(h, g[0], lbs[layer], w_in_a[layer])
            if sample:
                a, s = _hgrn_step(y, hg_norm[layer], state, layer)
            else:
                a, s = _hgrn_scan(y.reshape(b, l, -1), hg_norm[layer])
            states.append(s)
            qm_col = 4
        else:
            (y,) = _norm_matmul(h, g[0], w_in_b[layer - n_a], (2 * MIX,), "inproj_b")
            if sample:
                a = _sb_sample(y, cache_k, cache_v, page_table, sb_bias[layer - n_a])
            else:
                a = _sb_prompt(y.reshape(b, l, -1), sb_k.reshape(b, l, MIX), sb_v.reshape(b, l, MIX),
                               sb_bias[layer - n_a])
            qm_col = 1
        if sample:
            h = _mix_out_sample(h, a.reshape(m, MIX), y, qm_col, mem_k, mem_v, layer, w_o[layer], g[1])
        else:
            h = _mix_out_prompt(h.reshape(b, l, d), a.reshape(b, l, MIX), y.reshape(b, l, -1), qm_col,
                                mem_k[layer], mem_v[layer], w_o[layer], g[1]).reshape(m, d)
        h = _ffn(h, g[2], w_gu[layer], w_down[layer], g[3])
        if layer == n_a - 1:
            sb_k, sb_v = _norm_matmul(h, g_kv, w_kv, (MIX, MIX), "kv_proj")
    return (h.reshape(b, l, d), jnp.stack(states),
            sb_k.reshape(b, l, HEADS, DH), sb_v.reshape(b, l, HEADS, DH))


def kernel(x_prompt, x_sample, mem_prompt, cache_sb_k, cache_sb_v, cache_mem_k, cache_mem_v, state_hgrn,
           page_table, g_norm, w_in_a, hg_lb, hg_norm, w_in_b, sb_bias, g_kv, w_kv, w_mem_kv, w_o, w_gu, w_down):
    depth = g_norm.shape[0]
    wts = (g_norm, w_in_a.astype(BF16), _lower_bounds(hg_lb), hg_norm, w_in_b.astype(BF16), sb_bias, g_kv,
           w_kv.astype(BF16), w_o.astype(BF16), w_gu.astype(BF16), w_down.astype(BF16))
    bp, n_mem = mem_prompt.shape[:2]
    mk_p, mv_p = _memkv_proj(mem_prompt, w_mem_kv)
    y_p, s_p, k_p, v_p = _trunk(x_prompt, mk_p, mv_p, None, None, None, None, wts)
    ds = x_sample.shape[0]
    cmk = cache_mem_k.reshape(depth, ds, n_mem * HEADS, DH)
    cmv = cache_mem_v.reshape(depth, ds, n_mem * HEADS, DH)
    y_s, s_s, k_s, v_s = _trunk(x_sample, cmk, cmv, state_hgrn, cache_sb_k, cache_sb_v, page_table, wts)
    return (y_p, y_s, s_p, s_s.astype(state_hgrn.dtype), k_p, v_p, k_s, v_s,
            mk_p.reshape(depth, bp, n_mem, HEADS, DH), mv_p.reshape(depth, bp, n_mem, HEADS, DH))
```

```python
import functools

import numpy as np
import jax
import jax.numpy as jnp
from jax import lax
from jax.experimental import pallas as pl
from jax.experimental.pallas import tpu as pltpu

F32 = jnp.float32
BF16 = jnp.bfloat16
EPS = 1e-6
LOG2E = 1.4426950408889634
HEADS = 4
DH = 128
MIX = HEADS * DH
CHUNK = 64
SUB = 8
HGRN_GROUP = 4
SB_BLK = 128
SB_QTILE = 1024
SB_UNROLL = 8
V7X_VMEM_BYTES = 64 * 1024 * 1024
VMEM_LIMIT = V7X_VMEM_BYTES - 8 * 1024 * 1024


def _cp(*sem):
    return pltpu.CompilerParams(dimension_semantics=sem, vmem_limit_bytes=VMEM_LIMIT)


def _dot(a, b):
    return jnp.dot(a, b, preferred_element_type=F32)


def _dot_nt(a, b):
    return lax.dot_general(a, b, (((1,), (1,)), ((), ())), preferred_element_type=F32)


def _rms(x, g):
    return x * lax.rsqrt(jnp.mean(x * x, axis=-1, keepdims=True) + EPS) * g


def _sigmoid(x):
    return 1.0 / (1.0 + jnp.exp(-x))


def _log_sigmoid(x):
    return jnp.minimum(x, 0.0) - jnp.log1p(jnp.exp(-jnp.abs(x)))


def _resident(shape):
    return pl.BlockSpec(shape, lambda *_: (0,) * len(shape), pipeline_mode=pl.Buffered(1))


def _memkv_kernel(x_ref, w_ref, k_ref, v_ref):
    y = _dot(x_ref[...].astype(BF16), w_ref[...].astype(BF16))
    k_ref[...] = y[:, :MIX]
    v_ref[...] = y[:, MIX:]


def _memkv_proj(mem, w_mem_kv):
    depth = w_mem_kv.shape[0]
    b, n_mem, d = mem.shape
    rows = b * n_mem
    out = jax.ShapeDtypeStruct((depth, rows, MIX), F32)
    k, v = pl.pallas_call(
        _memkv_kernel,
        grid=(depth,),
        in_specs=[pl.BlockSpec((rows, d), lambda l: (0, 0)),
                  pl.BlockSpec((None, d, 2 * MIX), lambda l: (l, 0, 0))],
        out_specs=[pl.BlockSpec((None, rows, MIX), lambda l: (l, 0, 0))] * 2,
        out_shape=[out, out],
        compiler_params=_cp("arbitrary"),
        name="memkv_proj",
    )(mem.reshape(rows, d), w_mem_kv)
    return k.reshape(depth, b, n_mem, MIX), v.reshape(depth, b, n_mem, MIX)


def _norm_matmul_kernel(x_ref, g_ref, w_ref, *out_refs):
    xn = _rms(x_ref[...], g_ref[...]).astype(BF16)
    off = 0
    for o_ref in out_refs:
        n = o_ref.shape[-1]
        o_ref[...] = _dot(xn, w_ref[:, off:off + n])
        off += n


def _row_tile(m):
    return min(m, 512)


def _norm_matmul(x, g, w, widths, name):
    m, d = x.shape
    tm = _row_tile(m)
    return pl.pallas_call(
        _norm_matmul_kernel,
        grid=(m // tm,),
        in_specs=[pl.BlockSpec((tm, d), lambda i: (i, 0)),
                  _resident((1, d)),
                  _resident(w.shape)],
        out_specs=[pl.BlockSpec((tm, n), lambda i: (i, 0)) for n in widths],
        out_shape=[jax.ShapeDtypeStruct((m, n), F32) for n in widths],
        compiler_params=_cp("arbitrary"),
        name=name,
    )(x, g.reshape(1, d), w)


def _inproj_a_kernel(x_ref, g_ref, lb_ref, w_ref, o_ref):
    xn = _rms(x_ref[...], g_ref[...]).astype(BF16)
    q = _dot(xn, w_ref[:, 0:MIX])
    o_ref[:, 0:MIX] = q * _sigmoid(q)
    f = _dot(xn, w_ref[:, MIX:2 * MIX])
    lb = lb_ref[...]
    a = jnp.log(lb)
    b = jnp.log1p(-lb) + _log_sigmoid(f)
    o_ref[:, MIX:2 * MIX] = jnp.maximum(a, b) + jnp.log1p(jnp.exp(-jnp.abs(a - b)))
    o_ref[:, 2 * MIX:3 * MIX] = _dot(xn, w_ref[:, 2 * MIX:3 * MIX])
    gt = _dot(xn, w_ref[:, 3 * MIX:4 * MIX])
    o_ref[:, 3 * MIX:4 * MIX] = gt * _sigmoid(gt)
    o_ref[:, 4 * MIX:5 * MIX] = _dot(xn, w_ref[:, 4 * MIX:5 * MIX])


def _inproj_a(x, g, lb, w):
    m, d = x.shape
    tm = _row_tile(m)
    n = w.shape[1]
    return pl.pallas_call(
        _inproj_a_kernel,
        grid=(m // tm,),
        in_specs=[pl.BlockSpec((tm, d), lambda i: (i, 0)),
                  _resident((1, d)),
                  _resident((1, MIX)),
                  _resident(w.shape)],
        out_specs=pl.BlockSpec((tm, n), lambda i: (i, 0)),
        out_shape=jax.ShapeDtypeStruct((m, n), F32),
        compiler_params=_cp("arbitrary"),
        name="inproj_a",
    )(x, g.reshape(1, d), lb.reshape(1, MIX), w)


def _hgrn_consts():
    c = CHUNK
    t = np.arange(c)[:, None]
    j = np.arange(c)[None, :]
    mats = [j <= t]
    levels = []
    for m in (SUB, 2 * SUB, 4 * SUB):
        base = (t // (2 * m)) * (2 * m)
        lower = (t - base) >= m
        mats.append((lower & (j >= base + m) & (j <= t)) | (~lower & (j >= t + 1) & (j <= base + m - 1)))
        levels.append(((t // (2 * m)) == (j // (2 * m))) & ((t % (2 * m)) >= m) & ((j % (2 * m)) < m))
    mats.append(j >= t + 1)
    diags = [(j == t)]
    for d in range(1, SUB):
        mats.append(((t % SUB) >= d) & (j >= t - d + 1) & (j <= t))
        diags.append((j == t - d) & ((t % SUB) >= d))
    cm = np.concatenate(mats, 0).astype(np.float32)
    masks = np.stack(levels + diags).astype(np.float32)
    return jnp.asarray(cm, BF16), jnp.asarray(masks, F32)


def _hgrn_decays(lf, cm):
    hi = lf.astype(BF16)
    lo = (lf - hi.astype(F32)).astype(BF16)
    return jnp.exp(_dot(cm, hi) + _dot(cm, lo)), 1.0 - jnp.exp(lf)


def _hgrn_intra(q, k, x, mk_ref):
    c = CHUNK
    att = mk_ref[3] * jnp.sum(q * k, axis=1, keepdims=True)
    for li in range(3):
        a = x[c * (1 + li):c * (2 + li)]
        att = att + mk_ref[li] * _dot_nt((q * a).astype(BF16), (k * a).astype(BF16))
    for d in range(1, SUB):
        gd = x[c * (4 + d):c * (5 + d)]
        p = q * pltpu.roll(k, d, 0) * gd
        att = att + mk_ref[3 + d] * jnp.sum(p, axis=1, keepdims=True)
    return att


def _hgrn_state(q, k, v, x, att, st):
    c = CHUNK
    eb = x[0:c]
    o = _dot_nt((q * eb).astype(BF16), st.astype(BF16)) + _dot(att.astype(BF16), v.astype(BF16))
    kh = (k * x[4 * c:5 * c]).astype(BF16)
    return o, st * eb[c - 1:c, :] + _dot(v.T.astype(BF16), kh)


def _hgrn_scan_kernel(q_ref, lf_ref, v_ref, g_ref, gain_ref, cm_ref, mk_ref, on_ref, s_ref, st_ref, *, n_chunks):
    l = pl.program_id(1)

    @pl.when(l == 0)
    def _():
        st_ref[...] = jnp.zeros_like(st_ref)

    cm = cm_ref[...]
    gain = gain_ref[...]
    sts = [st_ref[hd] for hd in range(HEADS)]
    heads = [slice(hd * DH, (hd + 1) * DH) for hd in range(HEADS)]
    for c0 in range(0, n_chunks, HGRN_GROUP):
        tiles = [(slice(ci * CHUNK, (ci + 1) * CHUNK), hd) for ci in range(c0, c0 + HGRN_GROUP) for hd in range(HEADS)]
        qs = [q_ref[rows, heads[hd]] for rows, hd in tiles]
        xk = [_hgrn_decays(lf_ref[rows, heads[hd]], cm) for rows, hd in tiles]
        atts = [_hgrn_intra(qs[t], xk[t][1], xk[t][0], mk_ref) for t in range(len(tiles))]
        for t, (rows, hd) in enumerate(tiles):
            cols = heads[hd]
            o, sts[hd] = _hgrn_state(qs[t], xk[t][1], v_ref[rows, cols], xk[t][0], atts[t], sts[hd])
            on_ref[rows, cols] = _rms(o, gain) * g_ref[rows, cols]
    for hd in range(HEADS):
        st_ref[hd] = sts[hd]

    @pl.when(l == pl.num_programs(1) - 1)
    def _():
        for hd in range(HEADS):
            s_ref[hd] = sts[hd].T


def _hgrn_scan(y, gain, tokens_per_step=256):
    b, l, _ = y.shape
    t = min(tokens_per_step, l)
    assert l % t == 0 and t % CHUNK == 0
    cm, masks = _hgrn_consts()

    def col(group):
        return pl.BlockSpec((None, t, MIX), lambda bi, li: (bi, li, group))

    return pl.pallas_call(
        functools.partial(_hgrn_scan_kernel, n_chunks=t // CHUNK),
        grid=(b, l // t),
        in_specs=[col(0), col(1), col(2), col(3),
                  _resident((1, DH)), _resident(cm.shape), _resident(masks.shape)],
        out_specs=[pl.BlockSpec((None, t, MIX), lambda bi, li: (bi, li, 0)),
                   pl.BlockSpec((None, HEADS, DH, DH), lambda bi, li: (bi, 0, 0, 0))],
        out_shape=[jax.ShapeDtypeStruct((b, l, MIX), F32),
                   jax.ShapeDtypeStruct((b, HEADS, DH, DH), F32)],
        scratch_shapes=[pltpu.VMEM((HEADS, DH, DH), F32)],
        compiler_params=_cp("arbitrary", "arbitrary"),
        name="hgrn_scan",
    )(y, y, y, y, gain.reshape(1, DH), cm, masks)


def _hgrn_step_kernel(q_ref, lf_ref, v_ref, g_ref, gain_ref, s_ref, on_ref, so_ref, o_scr, *, n_samples):
    lf = lf_ref[...]
    f = jnp.exp(lf)
    ft = f.T
    kt = (1.0 - f).T
    qt = q_ref[...].T
    v = v_ref[...]
    for i in range(n_samples):
        sn = ft[:, i:i + 1] * s_ref[i] + kt[:, i:i + 1] * v[i:i + 1, :]
        so_ref[i] = sn
        o_scr[i:i + 1, :] = jnp.sum(qt[:, i:i + 1] * sn, axis=0, keepdims=True)
    on_ref[...] = _rms(o_scr[...], gain_ref[...]) * g_ref[...]


def _hgrn_step(y, gain, state, layer):
    n = y.shape[0]

    def col(group):
        return pl.BlockSpec((n, DH), lambda hi: (0, group * HEADS + hi))

    return pl.pallas_call(
        functools.partial(_hgrn_step_kernel, n_samples=n),
        grid=(HEADS,),
        in_specs=[col(0), col(1), col(2), col(3), _resident((1, DH)),
                  pl.BlockSpec((None, n, None, DH, DH), lambda hi: (layer, 0, hi, 0, 0))],
        out_specs=[pl.BlockSpec((n, DH), lambda hi: (0, hi)),
                   pl.BlockSpec((n, None, DH, DH), lambda hi: (0, hi, 0, 0))],
        out_shape=[jax.ShapeDtypeStruct((n, MIX), F32),
                   jax.ShapeDtypeStruct(state.shape[1:], F32)],
        scratch_shapes=[pltpu.VMEM((n, DH), F32)],
        compiler_params=_cp("arbitrary"),
        name="hgrn_step",
    )(y, y, y, y, gain.reshape(1, DH), state)


def _mix_out_prompt_kernel(h_ref, a_ref, qm_ref, mk_ref, mv_ref, wo_ref, g_ref, o_ref, *, scale):
    mix = _dot(a_ref[...].astype(BF16), wo_ref[0:MIX, :])
    for hd in range(HEADS):
        sl = slice(hd * DH, (hd + 1) * DH)
        s = _dot_nt((qm_ref[:, sl] * scale).astype(BF16), mk_ref[:, sl].astype(BF16))
        p = jnp.exp(s - jnp.max(s, axis=-1, keepdims=True))
        om = _dot(p.astype(BF16), mv_ref[:, sl].astype(BF16)) / jnp.sum(p, axis=-1, keepdims=True)
        mix = mix + _dot(om.astype(BF16), wo_ref[MIX + hd * DH:MIX + (hd + 1) * DH, :])
    o_ref[...] = h_ref[...] + _rms(mix, g_ref[...])


def _mix_out_prompt(h, a, y, qm_col, mem_k, mem_v, w_o, g):
    b, l, d = h.shape
    tm = _row_tile(l)
    n_mem = mem_k.shape[1]
    tok = lambda w, c=0: pl.BlockSpec((None, tm, w), lambda bi, li: (bi, li, c))
    mem = pl.BlockSpec((None, n_mem, MIX), lambda bi, li: (bi, 0, 0))
    return pl.pallas_call(
        functools.partial(_mix_out_prompt_kernel, scale=DH ** -0.5),
        grid=(b, l // tm),
        in_specs=[tok(d), tok(MIX), tok(MIX, qm_col), mem, mem, _resident(w_o.shape), _resident((1, d))],
        out_specs=tok(d),
        out_shape=jax.ShapeDtypeStruct(h.shape, F32),
        compiler_params=_cp("arbitrary", "arbitrary"),
        name="mix_out_prompt",
    )(h, a, y, mem_k, mem_v, w_o, g.reshape(1, d))


def _head_rows(x_row):
    rows = [x_row[:, hd * DH:(hd + 1) * DH] for hd in range(HEADS)]
    return jnp.concatenate(rows + [jnp.zeros((8 - HEADS, DH), x_row.dtype)], axis=0)


def _own_head(lanes):
    r = lax.broadcasted_iota(jnp.int32, (8, lanes), 0)
    c = lax.broadcasted_iota(jnp.int32, (8, lanes), 1)
    return jnp.bitwise_and(c, HEADS - 1) == r


def _mix_out_sample_kernel(h_ref, a_ref, qm_ref, mk_ref, mv_ref, wo_ref, g_ref, o_ref, om_scr, *, scale, tb):
    own = _own_head(mk_ref.shape[1])
    for i in range(tb):
        q4 = _head_rows(qm_ref[i:i + 1, :] * scale).astype(BF16)
        s = jnp.where(own, _dot_nt(q4, mk_ref[i].astype(BF16)), -1e30)
        p = jnp.exp(s - jnp.max(s, axis=-1, keepdims=True))
        o4 = _dot(p.astype(BF16), mv_ref[i].astype(BF16)) / jnp.sum(p, axis=-1, keepdims=True)
        for hd in range(HEADS):
            om_scr[i:i + 1, hd * DH:(hd + 1) * DH] = o4[hd:hd + 1, :]
    mix = _dot(a_ref[...].astype(BF16), wo_ref[0:MIX, :]) + _dot(om_scr[...].astype(BF16), wo_ref[MIX:, :])
    o_ref[...] = h_ref[...] + _rms(mix, g_ref[...])


def _mix_out_sample(h, a, y, qm_col, mem_k, mem_v, layer, w_o, g):
    n, d = h.shape
    tb = min(n, 8)
    rows = mem_k.shape[2]
    tok = lambda w, c=0: pl.BlockSpec((tb, w), lambda i: (i, c))
    mem = pl.BlockSpec((None, tb, rows, DH), lambda i: (layer, i, 0, 0))
    return pl.pallas_call(
        functools.partial(_mix_out_sample_kernel, scale=DH ** -0.5, tb=tb),
        grid=(n // tb,),
        in_specs=[tok(d), tok(MIX), tok(MIX, qm_col), mem, mem, _resident(w_o.shape), _resident((1, d))],
        out_specs=tok(d),
        out_shape=jax.ShapeDtypeStruct(h.shape, F32),
        scratch_shapes=[pltpu.VMEM((tb, MIX), F32)],
        compiler_params=_cp("arbitrary"),
        name="mix_out_sample",
    )(h, a, y, mem_k, mem_v, w_o, g.reshape(1, d))


def _ffn_kernel(h_ref, g_in_ref, wgu_ref, wd_ref, g_out_ref, o_ref, act_ref, *, d_ff, tc):
    x = h_ref[...]
    xn = _rms(x, g_in_ref[...]).astype(BF16)
    for c0 in range(0, d_ff, tc):
        gate = _dot(xn, wgu_ref[:, c0:c0 + tc])
        up = _dot(xn, wgu_ref[:, d_ff + c0:d_ff + c0 + tc])
        act_ref[:, c0:c0 + tc] = (gate * _sigmoid(gate) * up).astype(BF16)
    y = _dot(act_ref[...], wd_ref[...])
    o_ref[...] = x + _rms(y, g_out_ref[...])


def _ffn(h, g_in, w_gu, w_down, g_out):
    m, d = h.shape
    d_ff = w_down.shape[0]
    tm = _row_tile(m)
    tc = 256
    assert d_ff % tc == 0
    return pl.pallas_call(
        functools.partial(_ffn_kernel, d_ff=d_ff, tc=tc),
        grid=(m // tm,),
        in_specs=[pl.BlockSpec((tm, d), lambda i: (i, 0)),
                  _resident((1, d)), _resident(w_gu.shape), _resident(w_down.shape), _resident((1, d))],
        out_specs=pl.BlockSpec((tm, d), lambda i: (i, 0)),
        out_shape=jax.ShapeDtypeStruct((m, d), F32),
        scratch_shapes=[pltpu.VMEM((tm, d_ff), BF16)],
        compiler_params=_cp("arbitrary"),
        name="ffn",
    )(h, g_in.reshape(1, d), w_gu, w_down, g_out.reshape(1, d))


def _sb_consts(with_totals):
    s = np.arange(SB_BLK)
    u = (s[:, None] > s[None, :]).astype(np.float32)
    if with_totals:
        u = np.concatenate([u, np.ones((SB_BLK, SB_BLK), np.float32)], 1)
    return jnp.asarray(u, BF16)


def _log2_keep(zz):
    return jnp.minimum(zz, 0.0) - jnp.log2(1.0 + jnp.exp2(-jnp.abs(zz)))


def _sb_prompt_kernel(bias_ref, q_ref, k_ref, v_ref, u_ref, o_ref, *, scale, tq):
    hd = pl.program_id(1)
    i = pl.program_id(2)
    nsub = tq // SB_BLK
    nbias = bias_ref[hd] * (-LOG2E)
    qn = (q_ref[...] * (-scale * LOG2E)).astype(BF16)
    u = u_ref[...]
    row = lax.broadcasted_iota(jnp.int32, (tq, SB_BLK), 0)
    col = lax.broadcasted_iota(jnp.int32, (tq, SB_BLK), 1)

    def step(j, acc, c, mask):
        off = pl.multiple_of(j * SB_BLK, SB_BLK)
        kj = k_ref[pl.ds(off, SB_BLK), :].astype(BF16)
        vj = v_ref[pl.ds(off, SB_BLK), :].astype(BF16)
        zz = _dot_nt(qn, kj) + nbias
        lk = _log2_keep(zz)
        if mask is not None:
            lk = jnp.where(mask, lk, 0.0)
        w = jnp.exp2(lk - zz + c + _dot(lk.astype(BF16), u))
        if mask is not None:
            w = jnp.where(mask, w, 0.0)
        return acc + _dot(w.astype(BF16), vj), c + jnp.sum(lk, axis=1, keepdims=True)

    acc = jnp.zeros((tq, DH), F32)
    c = jnp.zeros((tq, 1), F32)
    for jj in reversed(range(nsub)):
        acc, c = step(i * nsub + jj, acc, c, col + jj * SB_BLK < row)
    def body(it, carry):
        for s in range(SB_UNROLL):
            carry = step(i * nsub - 1 - it * SB_UNROLL - s, carry[0], carry[1], None)
        return carry

    acc, c = lax.fori_loop(0, i * (nsub // SB_UNROLL), body, (acc, c))
    o_ref[...] = acc


def _sb_prompt(yb, k, v, bias):
    b, l, _ = yb.shape
    tq = min(SB_QTILE, l)
    assert l % tq == 0 and tq % SB_BLK == 0
    u = _sb_consts(False)
    kv = pl.BlockSpec((None, l, DH), lambda bi, hi, qi: (bi, 0, hi))
    qo = pl.BlockSpec((None, tq, DH), lambda bi, hi, qi: (bi, qi, hi))
    return pl.pallas_call(
        functools.partial(_sb_prompt_kernel, scale=DH ** -0.5, tq=tq),
        grid=(b, HEADS, l // tq),
        in_specs=[pl.BlockSpec(memory_space=pltpu.SMEM), qo, kv, kv, _resident(u.shape)],
        out_specs=qo,
        out_shape=jax.ShapeDtypeStruct((b, l, MIX), F32),
        compiler_params=_cp("arbitrary", "arbitrary", "arbitrary"),
        name="sb_prompt",
    )(bias, yb, k, v, u)


def _sb_sample_kernel(pt_ref, bias_ref, q_ref, *refs, n_pages, scale):
    k_refs = refs[:n_pages]
    v_refs = refs[n_pages:2 * n_pages]
    u_ref, o_ref = refs[2 * n_pages:]
    lanes = k_refs[0].shape[0]
    per_page = lanes // SB_BLK
    q4 = _head_rows(q_ref[...] * (-scale * LOG2E)).astype(BF16)
    own = _own_head(lanes)
    r = lax.broadcasted_iota(jnp.int32, (8, lanes), 0)
    nbias = jnp.zeros((8, lanes), F32)
    for hd in range(HEADS):
        nbias = jnp.where(r == hd, bias_ref[hd] * (-LOG2E), nbias)
    zzs, lks, blocks = [], [], []
    for j in range(n_pages):
        zz = _dot_nt(q4, k_refs[j][...].astype(BF16)) + nbias
        lk = jnp.where(own, _log2_keep(zz), 0.0)
        zzs.append(zz)
        lks.append(lk)
        blocks += [lk[:, cc * SB_BLK:(cc + 1) * SB_BLK] for cc in range(per_page)]
    stacked = jnp.concatenate(blocks, axis=0)
    hi = stacked.astype(BF16)
    lo = (stacked - hi.astype(F32)).astype(BF16)
    cs = _dot(hi, u_ref[...]) + _dot(lo, u_ref[...])
    c = jnp.zeros((8, SB_BLK), F32)
    tails = [None] * len(blocks)
    for gi in reversed(range(len(blocks))):
        blk = cs[8 * gi:8 * gi + 8]
        tails[gi] = c + blk[:, :SB_BLK]
        c = c + blk[:, SB_BLK:]
    acc = jnp.zeros((8, DH), F32)
    for j in range(n_pages):
        tail = jnp.concatenate(tails[j * per_page:(j + 1) * per_page], axis=1)
        w = jnp.where(own, jnp.exp2(lks[j] - zzs[j] + tail), 0.0)
        acc = acc + _dot(w.astype(BF16), v_refs[j][...].astype(BF16))
    for hd in range(HEADS):
        o_ref[:, hd * DH:(hd + 1) * DH] = acc[hd:hd + 1, :]


def _sb_sample(yb, cache_k, cache_v, page_table, bias):
    n = yb.shape[0]
    n_pool, page = cache_k.shape[:2]
    n_pages = page_table.shape[1]
    rows = page * HEADS
    assert rows % SB_BLK == 0
    u = _sb_consts(True)
    ck = cache_k.reshape(n_pool, rows, DH)
    cv = cache_v.reshape(n_pool, rows, DH)
    pages = [pl.BlockSpec((None, rows, DH), lambda i, pt, j=j: (pt[i, j], 0, 0)) for j in range(n_pages)]
    row = pl.BlockSpec((None, 1, MIX), lambda i, pt: (i, 0, 0))
    out = pl.pallas_call(
        functools.partial(_sb_sample_kernel, n_pages=n_pages, scale=DH ** -0.5),
        grid_spec=pltpu.PrefetchScalarGridSpec(
            num_scalar_prefetch=1,
            grid=(n,),
            in_specs=[pl.BlockSpec(memory_space=pltpu.SMEM), row] + pages + pages
                     + [pl.BlockSpec(u.shape, lambda i, pt: (0, 0))],
            out_specs=row),
        out_shape=jax.ShapeDtypeStruct((n, 1, MIX), F32),
        compiler_params=_cp("arbitrary"),
        name="sb_sample",
    )(page_table, bias, yb.reshape(n, 1, 2 * MIX), *([ck] * n_pages), *([cv] * n_pages), u)
    return out.reshape(n, MIX)


def _lower_bounds(lb_param):
    c = jnp.cumsum(jax.nn.softmax(lb_param.astype(F32), axis=0), axis=0)
    return c - c[0:1]


def _trunk(x, mem_k, mem_v, state, cache_k, cache_v, page_table, wts):
    g_norm, w_in_a, lbs, hg_norm, w_in_b, sb_bias, g_kv, w_kv, w_o, w_gu, w_down = wts
    b, l, d = x.shape
    sample = state is not None
    m = b * l
    depth = g_norm.shape[0]
    n_a = w_in_a.shape[0]
    h = x.reshape(m, d)
    states = []
    sb_k = sb_v = None
    for layer in range(depth):
        g = g_norm[layer]
        if layer < n_a:
            y = _inproj_a(h, g[0], lbs[layer], w_in_a[layer])
            if sample:
                a, s = _hgrn_step(y, hg_norm[layer], state, layer)
            else:
                a, s = _hgrn_scan(y.reshape(b, l, -1), hg_norm[layer])
            states.append(s)
            qm_col = 4
        else:
            (y,) = _norm_matmul(h, g[0], w_in_b[layer - n_a], (2 * MIX,), "inproj_b")
            if sample:
                a = _sb_sample(y, cache_k, cache_v, page_table, sb_bias[layer - n_a])
            else:
                a = _sb_prompt(y.reshape(b, l, -1), sb_k.reshape(b, l, MIX), sb_v.reshape(b, l, MIX),
                               sb_bias[layer - n_a])
            qm_col = 1
        if sample:
            h = _mix_out_sample(h, a.reshape(m, MIX), y, qm_col, mem_k, mem_v, layer, w_o[layer], g[1])
        else:
            h = _mix_out_prompt(h.reshape(b, l, d), a.reshape(b, l, MIX), y.reshape(b, l, -1), qm_col,
                                mem_k[layer], mem_v[layer], w_o[layer], g[1]).reshape(m, d)
        h = _ffn(h, g[2], w_gu[layer], w_down[layer], g[3])
        if layer == n_a - 1:
            sb_k, sb_v = _norm_matmul(h, g_kv, w_kv, (MIX, MIX), "kv_proj")
    return (h.reshape(b, l, d), jnp.stack(states),
            sb_k.reshape(b, l, HEADS, DH), sb_v.reshape(b, l, HEADS, DH))


def kernel(x_prompt, x_sample, mem_prompt, cache_sb_k, cache_sb_v, cache_mem_k, cache_mem_v, state_hgrn,
           page_table, g_norm, w_in_a, hg_lb, hg_norm, w_in_b, sb_bias, g_kv, w_kv, w_mem_kv, w_o, w_gu, w_down):
    depth = g_norm.shape[0]
    wts = (g_norm, w_in_a.astype(BF16), _lower_bounds(hg_lb), hg_norm, w_in_b.astype(BF16), sb_bias, g_kv,
           w_kv.astype(BF16), w_o.astype(BF16), w_gu.astype(BF16), w_down.astype(BF16))
    bp, n_mem = mem_prompt.shape[:2]
    mk_p, mv_p = _memkv_proj(mem_prompt, w_mem_kv)
    y_p, s_p, k_p, v_p = _trunk(x_prompt, mk_p, mv_p, None, None, None, None, wts)
    ds = x_sample.shape[0]
    cmk = cache_mem_k.reshape(depth, ds, n_mem * HEADS, DH)
    cmv = cache_mem_v.reshape(depth, ds, n_mem * HEADS, DH)
    y_s, s_s, k_s, v_s = _trunk(x_sample, cmk, cmv, state_hgrn, cache_sb_k, cache_sb_v, page_table, wts)
    return (y_p, y_s, s_p, s_s.astype(state_hgrn.dtype), k_p, v_p, k_s, v_s,
            mk_p.reshape(depth, bp, n_mem, HEADS, DH), mv_p.reshape(depth, bp, n_mem, HEADS, DH))
```
